```python
import jax, jax.numpy as jnp
from jax import lax
import numpy as np

D_MODEL = 1024
BATCH = 8
SEQ = 2048
DEPTH = 4

N_MEM = 256
RMS_EPS = 1e-6
N_SUBNORMS = 6

ML_HEADS = 4
ML_QK_DIM = D_MODEL // 2 // ML_HEADS
ML_V_DIM = D_MODEL // ML_HEADS
ML_CHUNK = 64
ML_GATE_CAP = 15.0
ML_IN_WIDTH = 2 * ML_HEADS * ML_QK_DIM + 2 * D_MODEL + 2 * ML_HEADS

SB_HEADS = 16
SB_HEAD_DIM = D_MODEL // SB_HEADS
SB_BLOCK = 128

XA_HEADS = 4
XA_HEAD_DIM = D_MODEL // XA_HEADS

D_FF = -(-8 * D_MODEL // (3 * 256)) * 256

N_ML_LAYERS = (DEPTH + 1) // 2
N_SB_LAYERS = DEPTH // 2

kernel_name = "hybrid_mlstm_stickbreak_memxattn_trunk"


def rms_norm(x, gain):
    x32 = x.astype(jnp.float32)
    y = x32 * lax.rsqrt(jnp.mean(x32 * x32, axis=-1, keepdims=True) + RMS_EPS)
    return (y * gain.astype(jnp.float32)).astype(x.dtype)


def mlstm_mixer(h, w_in, b_gate, head_gain, w_out):
    B, S, _ = h.shape
    H, dk, dv, L = ML_HEADS, ML_QK_DIM, ML_V_DIM, ML_CHUNK
    nc = S // L
    qk = H * dk
    proj = h @ w_in
    q, k, v, o, gates = jnp.split(proj, [qk, 2 * qk, 2 * qk + D_MODEL, 2 * qk + 2 * D_MODEL], axis=-1)
    gates = gates.astype(jnp.float32) + b_gate.astype(jnp.float32)
    gates = ML_GATE_CAP * jnp.tanh(gates / ML_GATE_CAP)
    ig = gates[..., :H]
    lf = jax.nn.log_sigmoid(gates[..., H:])

    def to_chunks(t, d):
        return t.astype(jnp.float32).reshape(B, nc, L, H, d).transpose(1, 0, 3, 2, 4)

    def gate_chunks(t):
        return t.reshape(B, nc, L, H).transpose(1, 0, 3, 2)

    qc = to_chunks(q, dk)
    kc = to_chunks(k, dk) * (dk ** -0.5)
    vc = to_chunks(v, dv)
    igc, lfc = gate_chunks(ig), gate_chunks(lf)
    causal = jnp.tril(jnp.ones((L, L), dtype=bool))

    def step(carry, inp):
        C, n, m = carry
        qb, kb, vb, igb, lfb = inp
        b = jnp.cumsum(lfb, axis=-1)
        log_d = jnp.where(causal, b[..., :, None] - b[..., None, :] + igb[..., None, :], -jnp.inf)
        m_inter = b + m[..., None]
        m_t = jnp.maximum(jnp.max(log_d, axis=-1), m_inter)
        d = jnp.exp(log_d - m_t[..., None])
        s = jnp.einsum('bhtd,bhsd->bhts', qb, kb) * d
        inter = jnp.exp(m_inter - m_t)
        num = jnp.einsum('bhts,bhsv->bhtv', s, vb) + inter[..., None] * jnp.einsum('bhvd,bhtd->bhtv', C, qb)
        den = jnp.sum(s, axis=-1) + inter * jnp.einsum('bhd,bhtd->bht', n, qb)
        h_out = num / jnp.maximum(jnp.abs(den), jnp.exp(-m_t))[..., None]
        b_last = b[..., -1]
        g = b_last[..., None] - b + igb
        m_new = jnp.maximum(b_last + m, jnp.max(g, axis=-1))
        w = jnp.exp(g - m_new[..., None])
        decay = jnp.exp(b_last + m - m_new)
        C = decay[..., None, None] * C + jnp.einsum('bhs,bhsv,bhsd->bhvd', w, vb, kb)
        n = decay[..., None] * n + jnp.einsum('bhs,bhsd->bhd', w, kb)
        return (C, n, m_new), h_out

    init = (jnp.zeros((B, H, dv, dk), jnp.float32),
            jnp.zeros((B, H, dk), jnp.float32),
            jnp.zeros((B, H), jnp.float32))
    _, hs = lax.scan(step, init, (qc, kc, vc, igc, lfc))
    hs = hs.transpose(1, 0, 3, 2, 4).reshape(B, S, H, dv)
    hs = rms_norm(hs, head_gain.reshape(H, dv))
    og = jax.nn.sigmoid(o.astype(jnp.float32)).reshape(B, S, H, dv)
    out = (hs * og).reshape(B, S, D_MODEL).astype(h.dtype)
    return out @ w_out


def stick_breaking_mixer(h, w_qkv, w_out):
    B, S, _ = h.shape
    H, dh = SB_HEADS, SB_HEAD_DIM
    qkv = (h @ w_qkv).reshape(B, S, 3, H, dh).astype(jnp.float32)
    q = qkv[:, :, 0].transpose(0, 2, 1, 3)
    k = qkv[:, :, 1].transpose(0, 2, 1, 3)
    v = qkv[:, :, 2].transpose(0, 2, 1, 3)
    scale = dh ** -0.5
    outs = []
    for blk in range(S // SB_BLOCK):
        q0 = blk * SB_BLOCK
        kend = q0 + SB_BLOCK
        qb = q[:, :, q0:kend]
        kb = k[:, :, :kend]
        vb = v[:, :, :kend]
        z = jnp.einsum('bhtd,bhsd->bhts', qb, kb) * scale
        t_idx = q0 + jnp.arange(SB_BLOCK)[:, None]
        s_idx = jnp.arange(kend)[None, :]
        strict = s_idx < t_idx
        log_1m = jnp.where(strict, jax.nn.log_sigmoid(-z), 0.0)
        suffix = lax.cumsum(log_1m, axis=3, reverse=True) - log_1m
        a = jnp.where(strict, jnp.exp(jax.nn.log_sigmoid(z) + suffix), 0.0)
        outs.append(jnp.einsum('bhts,bhsd->bhtd', a, vb))
    o = jnp.concatenate(outs, axis=2)
    o = o.transpose(0, 2, 1, 3).reshape(B, S, D_MODEL).astype(h.dtype)
    return o @ w_out


def memory_cross_attention(h, mem_n, w_q, w_kv, w_o):
    B, S, _ = h.shape
    q = (h @ w_q).reshape(B, S, XA_HEADS, XA_HEAD_DIM)
    kv = (mem_n @ w_kv).reshape(B, mem_n.shape[1], 2, XA_HEADS, XA_HEAD_DIM)
    k, v = kv[:, :, 0], kv[:, :, 1]
    s = jnp.einsum('bthd,bmhd->bhtm', q, k).astype(jnp.float32) * (XA_HEAD_DIM ** -0.5)
    p = jax.nn.softmax(s, axis=-1).astype(v.dtype)
    o = jnp.einsum('bhtm,bmhd->bthd', p, v).reshape(B, S, D_MODEL)
    return o @ w_o


def swiglu_ffn(h, w_gate_up, w_down):
    gate, up = jnp.split(h @ w_gate_up, 2, axis=-1)
    return (jax.nn.silu(gate) * up) @ w_down


def setup_inputs(seed: int = 0) -> dict:
    key = jax.random.key(seed)
    ks = jax.random.split(key, 16)

    def w(k, shape, fan_in):
        return jax.random.normal(k, shape, jnp.float32) * (fan_in ** -0.5)

    x = jax.random.normal(ks[0], (BATCH, SEQ, D_MODEL), jnp.float32)
    mem = jax.random.normal(ks[1], (BATCH, N_MEM, D_MODEL), jnp.float32)
    mem_norm_gain = 1.0 + 0.02 * jax.random.normal(ks[2], (D_MODEL,), jnp.float32)
    norm_gains = 1.0 + 0.02 * jax.random.normal(ks[3], (DEPTH, N_SUBNORMS, D_MODEL), jnp.float32)
    ml_w_in = w(ks[4], (N_ML_LAYERS, D_MODEL, ML_IN_WIDTH), D_MODEL)
    kb1, kb2 = jax.random.split(ks[5])
    b_in = 0.01 * jax.random.normal(kb1, (N_ML_LAYERS, ML_HEADS), jnp.float32)
    b_f = jnp.linspace(3.0, 6.0, ML_HEADS)[None, :] + 0.01 * jax.random.normal(kb2, (N_ML_LAYERS, ML_HEADS), jnp.float32)
    ml_b_gate = jnp.concatenate([b_in, b_f], axis=-1)
    ml_head_gain = 1.0 + 0.02 * jax.random.normal(ks[6], (N_ML_LAYERS, D_MODEL), jnp.float32)
    ml_w_out = w(ks[7], (N_ML_LAYERS, D_MODEL, D_MODEL), D_MODEL)
    sb_w_qkv = w(ks[8], (N_SB_LAYERS, D_MODEL, 3 * D_MODEL), D_MODEL)
    sb_w_out = w(ks[9], (N_SB_LAYERS, D_MODEL, D_MODEL), D_MODEL)
    xa_w_q = w(ks[10], (DEPTH, D_MODEL, D_MODEL), D_MODEL)
    xa_w_kv = w(ks[11], (DEPTH, D_MODEL, 2 * D_MODEL), D_MODEL)
    xa_w_o = w(ks[12], (DEPTH, D_MODEL, D_MODEL), D_MODEL)
    ffn_w_gate_up = w(ks[13], (DEPTH, D_MODEL, 2 * D_FF), D_MODEL)
    ffn_w_down = w(ks[14], (DEPTH, D_FF, D_MODEL), D_FF)
    return {"x": x, "mem": mem, "mem_norm_gain": mem_norm_gain, "norm_gains": norm_gains,
            "ml_w_in": ml_w_in, "ml_b_gate": ml_b_gate, "ml_head_gain": ml_head_gain, "ml_w_out": ml_w_out,
            "sb_w_qkv": sb_w_qkv, "sb_w_out": sb_w_out,
            "xa_w_q": xa_w_q, "xa_w_kv": xa_w_kv, "xa_w_o": xa_w_o,
            "ffn_w_gate_up": ffn_w_gate_up, "ffn_w_down": ffn_w_down}


def reference(x, mem, mem_norm_gain, norm_gains, ml_w_in, ml_b_gate, ml_head_gain, ml_w_out,
              sb_w_qkv, sb_w_out, xa_w_q, xa_w_kv, xa_w_o, ffn_w_gate_up, ffn_w_down):
    mem_n = rms_norm(mem, mem_norm_gain)
    for layer in range(DEPTH):
        g = norm_gains[layer]
        j = layer // 2
        hn = rms_norm(x, g[0])
        if layer % 2 == 0:
            y = mlstm_mixer(hn, ml_w_in[j], ml_b_gate[j], ml_head_gain[j], ml_w_out[j])
        else:
            y = stick_breaking_mixer(hn, sb_w_qkv[j], sb_w_out[j])
        x = x + rms_norm(y, g[1])
        y = memory_cross_attention(rms_norm(x, g[2]), mem_n, xa_w_q[layer], xa_w_kv[layer], xa_w_o[layer])
        x = x + rms_norm(y, g[3])
        y = swiglu_ffn(rms_norm(x, g[4]), ffn_w_gate_up[layer], ffn_w_down[layer])
        x = x + rms_norm(y, g[5])
    return x
```

```python
import functools

import jax
import jax.numpy as jnp
from jax import lax
from jax.experimental import pallas as pl
from jax.experimental.pallas import tpu as pltpu

F32 = jnp.float32
BF16 = jnp.bfloat16

RMS_EPS = 1e-6
ML_HEADS = 4
ML_QK_DIM = 128
ML_V_DIM = 256
ML_GATE_CAP = 15.0
SB_HEADS = 16
SB_HEAD_DIM = 64
XA_HEADS = 4
XA_HEAD_DIM = 256

LANES = 128
ROW_TILE = 256
ML_CHUNK = 128
SB_TILE = 128
FF_CHUNK = 256
VMEM_LIMIT = 56 * 1024 * 1024
NEG_BIG = -1e30


def _rms(x, g):
    ms = jnp.mean(x * x, axis=-1, keepdims=True)
    return x * lax.rsqrt(ms + RMS_EPS) * g


def _sigmoid(x):
    return 1.0 / (1.0 + jnp.exp(-x))


def _log_sigmoid(x):
    return jnp.minimum(x, 0.0) - jnp.log1p(jnp.exp(-jnp.abs(x)))


def _dot(a, b):
    return jnp.dot(a, b, preferred_element_type=F32)


def _dot_nt(a, b):
    return lax.dot_general(a, b, (((1,), (1,)), ((), ())), preferred_element_type=F32)


def _params(*sem):
    return pltpu.CompilerParams(dimension_semantics=sem, vmem_limit_bytes=VMEM_LIMIT)


def _const_spec(shape):
    return pl.BlockSpec(shape, lambda *_: (0,) * len(shape))


def _ml_inproj_kernel(x_ref, g_ref, wq_ref, wk_ref, wv_ref, wo_ref, wg_ref, bg_ref,
                      q_ref, k_ref, v_ref, o_ref, gate_ref):
    xn = _rms(x_ref[...], g_ref[...]).astype(BF16)
    q_ref[...] = _dot(xn, wq_ref[...]).astype(BF16)
    k_ref[...] = (_dot(xn, wk_ref[...]) * (ML_QK_DIM ** -0.5)).astype(BF16)
    v_ref[...] = _dot(xn, wv_ref[...]).astype(BF16)
    o_ref[...] = _dot(xn, wo_ref[...]).astype(BF16)
    gates = _dot(xn, wg_ref[...]) + bg_ref[...]
    gates = ML_GATE_CAP * jnp.tanh(gates / ML_GATE_CAP)
    lane = lax.broadcasted_iota(jnp.int32, gates.shape, 1)
    gate_ref[...] = jnp.where(lane < ML_HEADS, gates, _log_sigmoid(gates))


def _ml_inproj(x, g, wq, wk, wv, wo, wg, bg):
    m, d = x.shape
    tm = ROW_TILE
    row = lambda n: pl.BlockSpec((tm, n), lambda i: (i, 0))
    return pl.pallas_call(
        _ml_inproj_kernel,
        grid=(m // tm,),
        in_specs=[row(d), _const_spec((1, d)), _const_spec(wq.shape), _const_spec(wk.shape),
                  _const_spec(wv.shape), _const_spec(wo.shape), _const_spec(wg.shape),
                  _const_spec((1, LANES))],
        out_specs=[row(wq.shape[1]), row(wk.shape[1]), row(wv.shape[1]), row(wo.shape[1]), row(LANES)],
        out_shape=[jax.ShapeDtypeStruct((m, wq.shape[1]), BF16),
                   jax.ShapeDtypeStruct((m, wk.shape[1]), BF16),
                   jax.ShapeDtypeStruct((m, wv.shape[1]), BF16),
                   jax.ShapeDtypeStruct((m, wo.shape[1]), BF16),
                   jax.ShapeDtypeStruct((m, LANES), F32)],
        compiler_params=_params("parallel"),
        name="ml_inproj",
    )(x, g, wq, wk, wv, wo, wg, bg)


def _mlstm_kernel(q_ref, k_ref, v_ref, o_ref, gc_ref, gr_ref, hg_ref, out_ref, ct_ref, n_ref, m_ref):
    h = pl.program_id(1)
    c = pl.program_id(2)
    L = ML_CHUNK

    @pl.when(c == 0)
    def _():
        ct_ref[...] = jnp.zeros_like(ct_ref)
        n_ref[...] = jnp.zeros_like(n_ref)
        m_ref[...] = jnp.zeros_like(m_ref)

    q = q_ref[0]
    k = k_ref[0]
    v = v_ref[0]

    gc = gc_ref[0]
    lane = lax.broadcasted_iota(jnp.int32, gc.shape, 1)
    ig_col = jnp.sum(jnp.where(lane == h, gc, 0.0), axis=1, keepdims=True)
    lf_col = jnp.sum(jnp.where(lane == h + ML_HEADS, gc, 0.0), axis=1, keepdims=True)
    gr = gr_ref[0]
    sub = lax.broadcasted_iota(jnp.int32, gr.shape, 0)
    ig_row = jnp.sum(jnp.where(sub == h, gr, 0.0), axis=0, keepdims=True)
    lf_row = jnp.sum(jnp.where(sub == h + ML_HEADS, gr, 0.0), axis=0, keepdims=True)

    t_idx = lax.broadcasted_iota(jnp.int32, (L, L), 0)
    s_idx = lax.broadcasted_iota(jnp.int32, (L, L), 1)
    causal = s_idx <= t_idx
    b_col = jnp.sum(jnp.where(causal, lf_row, 0.0), axis=1, keepdims=True)
    b_row = jnp.sum(jnp.where(t_idx <= s_idx, lf_col, 0.0), axis=0, keepdims=True)
    a_row = ig_row - b_row
    a_col = ig_col - b_col

    m_prev = m_ref[...]
    mx_col = jnp.maximum(jnp.max(jnp.where(causal, a_row, NEG_BIG), axis=1, keepdims=True), m_prev)
    mx_last = mx_col[L - 1:L, :]
    b_last = b_col[L - 1:L, :]

    dmat = jnp.where(causal, jnp.exp(jnp.minimum(a_row - mx_col, 0.0)), 0.0)
    sd = _dot_nt(q, k) * dmat
    inter = jnp.exp(m_prev - mx_col)
    ct = ct_ref[...]
    num = _dot(sd.astype(BF16), v) + inter * _dot(q, ct.astype(BF16))
    nq = jnp.sum(q.astype(F32) * n_ref[...], axis=1, keepdims=True)
    den = jnp.sum(sd, axis=1, keepdims=True) + inter * nq
    hh = num / jnp.maximum(jnp.abs(den), jnp.exp(-(b_col + mx_col)))

    hn = _rms(hh, hg_ref[0])
    out_ref[0] = (hn * _sigmoid(o_ref[0].astype(F32))).astype(out_ref.dtype)

    w_col = jnp.exp(a_col - mx_last)
    decay = jnp.exp(m_prev - mx_last)
    kw = k.astype(F32) * w_col
    ct_ref[...] = decay * ct + _dot(kw.T.astype(BF16), v)
    n_ref[...] = decay * n_ref[...] + jnp.sum(kw, axis=0, keepdims=True)
    m_ref[...] = b_last + mx_last


def _mlstm_core(q, k, v, o, gates_col, gates_row, head_gain):
    b, s, _ = v.shape
    L = ML_CHUNK
    return pl.pallas_call(
        _mlstm_kernel,
        grid=(b, ML_HEADS, s // L),
        in_specs=[
            pl.BlockSpec((1, L, ML_QK_DIM), lambda bi, hi, ci: (bi, ci, hi)),
            pl.BlockSpec((1, L, ML_QK_DIM), lambda bi, hi, ci: (bi, ci, hi)),
            pl.BlockSpec((1, L, ML_V_DIM), lambda bi, hi, ci: (bi, ci, hi)),
            pl.BlockSpec((1, L, ML_V_DIM), lambda bi, hi, ci: (bi, ci, hi)),
            pl.BlockSpec((1, L, LANES), lambda bi, hi, ci: (bi, ci, 0)),
            pl.BlockSpec((1, 2 * ML_HEADS, L), lambda bi, hi, ci: (bi, 0, ci)),
            pl.BlockSpec((1, 1, ML_V_DIM), lambda bi, hi, ci: (hi, 0, 0)),
        ],
        out_specs=pl.BlockSpec((1, L, ML_V_DIM), lambda bi, hi, ci: (bi, ci, hi)),
        out_shape=jax.ShapeDtypeStruct(v.shape, BF16),
        scratch_shapes=[pltpu.VMEM((ML_QK_DIM, ML_V_DIM), F32),
                        pltpu.VMEM((1, ML_QK_DIM), F32),
                        pltpu.VMEM((1, 1), F32)],
        compiler_params=_params("parallel", "parallel", "arbitrary"),
        name="mlstm_core",
    )(q, k, v, o, gates_col, gates_row, head_gain)


def _sb_inproj_kernel(x_ref, g_ref, wq_ref, wk_ref, wv_ref, q_ref, k_ref, v_ref):
    xn = _rms(x_ref[...], g_ref[...]).astype(BF16)
    q_ref[...] = (_dot(xn, wq_ref[...]) * (SB_HEAD_DIM ** -0.5)).astype(BF16)
    k_ref[...] = _dot(xn, wk_ref[...]).astype(BF16)
    v_ref[...] = _dot(xn, wv_ref[...]).astype(BF16)


def _sb_inproj(x, g, wq, wk, wv):
    m, d = x.shape
    tm = ROW_TILE
    row = lambda n: pl.BlockSpec((tm, n), lambda i: (i, 0))
    return pl.pallas_call(
        _sb_inproj_kernel,
        grid=(m // tm,),
        in_specs=[row(d), _const_spec((1, d)), _const_spec(wq.shape), _const_spec(wk.shape),
                  _const_spec(wv.shape)],
        out_specs=[row(d), row(d), row(d)],
        out_shape=[jax.ShapeDtypeStruct((m, d), BF16)] * 3,
        compiler_params=_params("parallel"),
        name="sb_inproj",
    )(x, g, wq, wk, wv)


def _sb_kernel(q_ref, k_ref, v_ref, o_ref):
    qi = pl.program_id(2)
    T = SB_TILE
    q = q_ref[0]
    lane = lax.broadcasted_iota(jnp.int32, q.shape, 1)
    first = lane < SB_HEAD_DIM
    zero = jnp.zeros_like(q)
    q_heads = (jnp.where(first, q, zero), jnp.where(first, zero, q))

    row = lax.broadcasted_iota(jnp.int32, (T, T), 0)
    col = lax.broadcasted_iota(jnp.int32, (T, T), 1)
    suffix_ones = jnp.where(row >= col, 1.0, 0.0).astype(BF16)
    strict = col < row

    def tile(qh, kb, vb, carry, diagonal):
        z = _dot_nt(qh, kb)
        log_1m = -(jnp.maximum(z, 0.0) + jnp.log1p(jnp.exp(-jnp.abs(z))))
        if diagonal:
            log_1m = jnp.where(strict, log_1m, 0.0)
        hi = log_1m.astype(BF16)
        lo = (log_1m - hi.astype(F32)).astype(BF16)
        cum = _dot(hi, suffix_ones) + _dot(lo, suffix_ones)
        a = jnp.exp(z + cum + carry)
        if diagonal:
            a = jnp.where(strict, a, 0.0)
        return _dot(a.astype(BF16), vb), carry + cum[:, 0:1]

    start = pl.multiple_of(qi * T, T)
    kd = k_ref[0, pl.ds(start, T), :]
    vd = v_ref[0, pl.ds(start, T), :]
    c_init = jnp.zeros((T, 1), F32)
    acc0, c0 = tile(q_heads[0], kd, vd, c_init, True)
    acc1, c1 = tile(q_heads[1], kd, vd, c_init, True)

    def body(j, carry):
        acc0, acc1, c0, c1 = carry
        off = pl.multiple_of((qi - 1 - j) * T, T)
        kb = k_ref[0, pl.ds(off, T), :]
        vb = v_ref[0, pl.ds(off, T), :]
        p0, c0 = tile(q_heads[0], kb, vb, c0, False)
        p1, c1 = tile(q_heads[1], kb, vb, c1, False)
        return acc0 + p0, acc1 + p1, c0, c1

    acc0, acc1, _, _ = lax.fori_loop(0, qi, body, (acc0, acc1, c0, c1))
    o_ref[0] = jnp.where(first, acc0, acc1).astype(o_ref.dtype)


def _sb_core(q, k, v):
    b, s, d = q.shape
    T = SB_TILE
    return pl.pallas_call(
        _sb_kernel,
        grid=(b, d // LANES, s // T),
        in_specs=[pl.BlockSpec((1, T, LANES), lambda bi, hi, ti: (bi, ti, hi)),
                  pl.BlockSpec((1, s, LANES), lambda bi, hi, ti: (bi, 0, hi)),
                  pl.BlockSpec((1, s, LANES), lambda bi, hi, ti: (bi, 0, hi))],
        out_specs=pl.BlockSpec((1, T, LANES), lambda bi, hi, ti: (bi, ti, hi)),
        out_shape=jax.ShapeDtypeStruct(q.shape, BF16),
        compiler_params=_params("parallel", "parallel", "arbitrary"),
        name="sb_core",
    )(q, k, v)


def _outproj_kernel(a_ref, w_ref, g_ref, x_ref, o_ref):
    y = _dot(a_ref[...], w_ref[...])
    o_ref[...] = x_ref[...] + _rms(y, g_ref[...])


def _outproj(a, w, g, x):
    m, d = x.shape
    kdim = a.shape[1]
    tm = ROW_TILE
    return pl.pallas_call(
        _outproj_kernel,
        grid=(m // tm,),
        in_specs=[pl.BlockSpec((tm, kdim), lambda i: (i, 0)), _const_spec(w.shape), _const_spec((1, d)),
                  pl.BlockSpec((tm, d), lambda i: (i, 0))],
        out_specs=pl.BlockSpec((tm, d), lambda i: (i, 0)),
        out_shape=jax.ShapeDtypeStruct((m, d), F32),
        compiler_params=_params("parallel"),
        name="outproj",
    )(a, w, g, x)


def _memkv_kernel(mem_ref, g_ref, w_ref, o_ref):
    mn = _rms(mem_ref[...], g_ref[...]).astype(BF16)
    o_ref[0] = _dot(mn, w_ref[0]).astype(o_ref.dtype)


def _memkv(mem, g, w_kv):
    m, d = mem.shape
    depth, _, n = w_kv.shape
    tm = ROW_TILE
    return pl.pallas_call(
        _memkv_kernel,
        grid=(depth, m // tm),
        in_specs=[pl.BlockSpec((tm, d), lambda l, i: (i, 0)), _const_spec((1, d)),
                  pl.BlockSpec((1, d, n), lambda l, i: (l, 0, 0))],
        out_specs=pl.BlockSpec((1, tm, n), lambda l, i: (l, i, 0)),
        out_shape=jax.ShapeDtypeStruct((depth, m, n), BF16),
        compiler_params=_params("parallel", "parallel"),
        name="memkv",
    )(mem, g, w_kv)


def _xattn_kernel(x_ref, gpre_ref, gpost_ref, wq_ref, kv_ref, wo_ref, o_ref):
    x = x_ref[0]
    xn = _rms(x, gpre_ref[...]).astype(BF16)
    q = (_dot(xn, wq_ref[...]) * (XA_HEAD_DIM ** -0.5)).astype(BF16)
    d = XA_HEADS * XA_HEAD_DIM
    heads = []
    for h in range(XA_HEADS):
        lo, hi = h * XA_HEAD_DIM, (h + 1) * XA_HEAD_DIM
        s = _dot_nt(q[:, lo:hi], kv_ref[0, 0, :, lo:hi])
        p = jnp.exp(s - jnp.max(s, axis=-1, keepdims=True))
        denom = jnp.sum(p, axis=-1, keepdims=True)
        heads.append((_dot(p.astype(BF16), kv_ref[0, 0, :, d + lo:d + hi]) / denom).astype(BF16))
    attn = jnp.concatenate(heads, axis=-1)
    y = _dot(attn, wo_ref[...])
    o_ref[0] = x + _rms(y, gpost_ref[...])


def _xattn(x, gpre, gpost, wq, kv, layer, wo):
    b, s, d = x.shape
    n_mem = kv.shape[2]
    tm = ROW_TILE
    return pl.pallas_call(
        _xattn_kernel,
        grid=(b, s // tm),
        in_specs=[pl.BlockSpec((1, tm, d), lambda bi, i: (bi, i, 0)), _const_spec((1, d)),
                  _const_spec((1, d)), _const_spec(wq.shape),
                  pl.BlockSpec((1, 1, n_mem, 2 * d), lambda bi, i: (layer, bi, 0, 0)),
                  _const_spec(wo.shape)],
        out_specs=pl.BlockSpec((1, tm, d), lambda bi, i: (bi, i, 0)),
        out_shape=jax.ShapeDtypeStruct(x.shape, F32),
        compiler_params=_params("parallel", "parallel"),
        name="xattn",
    )(x, gpre, gpost, wq, kv, wo)


def _ffn_kernel(x_ref, gpre_ref, gpost_ref, wg_ref, wu_ref, wd_ref, o_ref):
    x = x_ref[...]
    xn = _rms(x, gpre_ref[...]).astype(BF16)
    d_ff = wg_ref.shape[1]
    y = jnp.zeros(x.shape, F32)
    for c0 in range(0, d_ff, FF_CHUNK):
        gate = _dot(xn, wg_ref[:, c0:c0 + FF_CHUNK])
        up = _dot(xn, wu_ref[:, c0:c0 + FF_CHUNK])
        hid = (gate * _sigmoid(gate) * up).astype(BF16)
        y = y + _dot(hid, wd_ref[c0:c0 + FF_CHUNK, :])
    o_ref[...] = x + _rms(y, gpost_ref[...])


def _ffn(x, gpre, gpost, wg, wu, wd):
    m, d = x.shape
    tm = ROW_TILE
    return pl.pallas_call(
        _ffn_kernel,
        grid=(m // tm,),
        in_specs=[pl.BlockSpec((tm, d), lambda i: (i, 0)), _const_spec((1, d)), _const_spec((1, d)),
                  _const_spec(wg.shape), _const_spec(wu.shape), _const_spec(wd.shape)],
        out_specs=pl.BlockSpec((tm, d), lambda i: (i, 0)),
        out_shape=jax.ShapeDtypeStruct((m, d), F32),
        compiler_params=_params("parallel"),
        name="ffn",
    )(x, gpre, gpost, wg, wu, wd)


def kernel(x, mem, mem_norm_gain, norm_gains, ml_w_in, ml_b_gate, ml_head_gain, ml_w_out, sb_w_qkv, sb_w_out,
           xa_w_q, xa_w_kv, xa_w_o, ffn_w_gate_up, ffn_w_down):
    b, s, d = x.shape
    depth = norm_gains.shape[0]
    n_mem = mem.shape[1]
    m = b * s
    d_ff = ffn_w_down.shape[1]
    qk = ML_HEADS * ML_QK_DIM
    n_gates = 2 * ML_HEADS
    gains = norm_gains.reshape(depth, norm_gains.shape[1], 1, d)

    kv = _memkv(mem.reshape(b * n_mem, d), mem_norm_gain.reshape(1, d), xa_w_kv.astype(BF16))
    kv = kv.reshape(depth, b, n_mem, 2 * d)

    xf = x.reshape(m, d)
    for layer in range(depth):
        g = gains[layer]
        j = layer // 2
        if layer % 2 == 0:
            w_in = ml_w_in[j]
            wg = jnp.pad(w_in[:, 2 * qk + 2 * d:], ((0, 0), (0, LANES - n_gates))).astype(BF16)
            bg = jnp.pad(ml_b_gate[j], (0, LANES - n_gates)).reshape(1, LANES)
            q, k, v, o, gates = _ml_inproj(
                xf, g[0], w_in[:, :qk].astype(BF16), w_in[:, qk:2 * qk].astype(BF16),
                w_in[:, 2 * qk:2 * qk + d].astype(BF16), w_in[:, 2 * qk + d:2 * qk + 2 * d].astype(BF16), wg, bg)
            gates = gates.reshape(b, s, LANES)
            gates_row = jnp.swapaxes(gates[:, :, :n_gates], 1, 2)
            mixed = _mlstm_core(q.reshape(b, s, qk), k.reshape(b, s, qk), v.reshape(b, s, d),
                                o.reshape(b, s, d), gates, gates_row,
                                ml_head_gain[j].reshape(ML_HEADS, 1, ML_V_DIM))
            xf = _outproj(mixed.reshape(m, d), ml_w_out[j].astype(BF16), g[1], xf)
        else:
            w = sb_w_qkv[j]
            q, k, v = _sb_inproj(xf, g[0], w[:, :d].astype(BF16), w[:, d:2 * d].astype(BF16),
                                 w[:, 2 * d:].astype(BF16))
            mixed = _sb_core(q.reshape(b, s, d), k.reshape(b, s, d), v.reshape(b, s, d))
            xf = _outproj(mixed.reshape(m, d), sb_w_out[j].astype(BF16), g[1], xf)
        xf = _xattn(xf.reshape(b, s, d), g[2], g[3], xa_w_q[layer].astype(BF16), kv, layer,
                    xa_w_o[layer].astype(BF16)).reshape(m, d)
        w_gu = ffn_w_gate_up[layer]
        xf = _ffn(xf, g[4], g[5], w_gu[:, :d_ff].astype(BF16), w_gu[:, d_ff:].astype(BF16),
                  ffn_w_down[layer].astype(BF16))
    return xf.reshape(b, s, d)
```

```python
import functools

import jax
import jax.numpy as jnp
from jax import lax
from jax.experimental import pallas as pl
from jax.experimental.pallas import tpu as pltpu

F32 = jnp.float32
BF16 = jnp.bfloat16

RMS_EPS = 1e-6
ML_HEADS = 4
ML_QK_DIM = 128
ML_V_DIM = 256
ML_GATE_CAP = 15.0
SB_HEADS = 16
SB_HEAD_DIM = 64
XA_HEADS = 4
XA_HEAD_DIM = 256

LANES = 128
ROW_TILE = 256
ML_CHUNK = 128
SB_TILE = 256
FF_CHUNK = 256
VMEM_LIMIT = 56 * 1024 * 1024
NEG_BIG = -1e30
LOG2_E = 1.4426950408889634


def _rms(x, g):
    ms = jnp.mean(x * x, axis=-1, keepdims=True)
    return x * lax.rsqrt(ms + RMS_EPS) * g


def _sigmoid(x):
    return 1.0 / (1.0 + jnp.exp(-x))


def _log_sigmoid(x):
    return jnp.minimum(x, 0.0) - jnp.log1p(jnp.exp(-jnp.abs(x)))


def _dot(a, b):
    return jnp.dot(a, b, preferred_element_type=F32)


def _dot_nt(a, b):
    return lax.dot_general(a, b, (((1,), (1,)), ((), ())), preferred_element_type=F32)


def _params(*sem):
    return pltpu.CompilerParams(dimension_semantics=sem, vmem_limit_bytes=VMEM_LIMIT)


def _const_spec(shape):
    return pl.BlockSpec(shape, lambda *_: (0,) * len(shape))


def _ml_inproj_kernel(x_ref, g_ref, wq_ref, wk_ref, wv_ref, wo_ref, wg_ref, bg_ref,
                      q_ref, k_ref, v_ref, o_ref, gate_ref):
    xn = _rms(x_ref[...], g_ref[...]).astype(BF16)
    q_ref[...] = _dot(xn, wq_ref[...]).astype(BF16)
    k_ref[...] = (_dot(xn, wk_ref[...]) * (ML_QK_DIM ** -0.5)).astype(BF16)
    v_ref[...] = _dot(xn, wv_ref[...]).astype(BF16)
    o_ref[...] = _dot(xn, wo_ref[...]).astype(BF16)
    gates = _dot(xn, wg_ref[...]) + bg_ref[...]
    gates = ML_GATE_CAP * jnp.tanh(gates / ML_GATE_CAP)
    lane = lax.broadcasted_iota(jnp.int32, gates.shape, 1)
    gate_ref[...] = jnp.where(lane < ML_HEADS, gates, _log_sigmoid(gates))


def _ml_inproj(x, g, wq, wk, wv, wo, wg, bg):
    m, d = x.shape
    tm = ROW_TILE
    row = lambda n: pl.BlockSpec((tm, n), lambda i: (i, 0))
    return pl.pallas_call(
        _ml_inproj_kernel,
        grid=(m // tm,),
        in_specs=[row(d), _const_spec((1, d)), _const_spec(wq.shape), _const_spec(wk.shape),
                  _const_spec(wv.shape), _const_spec(wo.shape), _const_spec(wg.shape),
                  _const_spec((1, LANES))],
        out_specs=[row(wq.shape[1]), row(wk.shape[1]), row(wv.shape[1]), row(wo.shape[1]), row(LANES)],
        out_shape=[jax.ShapeDtypeStruct((m, wq.shape[1]), BF16),
                   jax.ShapeDtypeStruct((m, wk.shape[1]), BF16),
                   jax.ShapeDtypeStruct((m, wv.shape[1]), BF16),
                   jax.ShapeDtypeStruct((m, wo.shape[1]), BF16),
                   jax.ShapeDtypeStruct((m, LANES), F32)],
        compiler_params=_params("parallel"),
        name="ml_inproj",
    )(x, g, wq, wk, wv, wo, wg, bg)


def _mlstm_kernel(q_ref, k_ref, v_ref, o_ref, gc_ref, gr_ref, hg_ref, out_ref, ct_ref, n_ref, m_ref):
    h = pl.program_id(1)
    c = pl.program_id(2)
    L = ML_CHUNK

    @pl.when(c == 0)
    def _():
        ct_ref[...] = jnp.zeros_like(ct_ref)
        n_ref[...] = jnp.zeros_like(n_ref)
        m_ref[...] = jnp.zeros_like(m_ref)

    q = q_ref[0]
    k = k_ref[0]
    v = v_ref[0]

    gc = gc_ref[0]
    lane = lax.broadcasted_iota(jnp.int32, gc.shape, 1)
    ig_col = jnp.sum(jnp.where(lane == h, gc, 0.0), axis=1, keepdims=True)
    lf_col = jnp.sum(jnp.where(lane == h + ML_HEADS, gc, 0.0), axis=1, keepdims=True)
    gr = gr_ref[0]
    sub = lax.broadcasted_iota(jnp.int32, gr.shape, 0)
    ig_row = jnp.sum(jnp.where(sub == h, gr, 0.0), axis=0, keepdims=True)
    lf_row = jnp.sum(jnp.where(sub == h + ML_HEADS, gr, 0.0), axis=0, keepdims=True)

    t_idx = lax.broadcasted_iota(jnp.int32, (L, L), 0)
    s_idx = lax.broadcasted_iota(jnp.int32, (L, L), 1)
    causal = s_idx <= t_idx
    b_col = jnp.sum(jnp.where(causal, lf_row, 0.0), axis=1, keepdims=True)
    b_row = jnp.sum(jnp.where(t_idx <= s_idx, lf_col, 0.0), axis=0, keepdims=True)
    a_row = ig_row - b_row
    a_col = ig_col - b_col

    m_prev = m_ref[...]
    mx_col = jnp.maximum(jnp.max(jnp.where(causal, a_row, NEG_BIG), axis=1, keepdims=True), m_prev)
    mx_last = mx_col[L - 1:L, :]
    b_last = b_col[L - 1:L, :]

    dmat = jnp.where(causal, jnp.exp(jnp.minimum(a_row - mx_col, 0.0)), 0.0)
    sd = _dot_nt(q, k) * dmat
    inter = jnp.exp(m_prev - mx_col)
    ct = ct_ref[...]
    num = _dot(sd.astype(BF16), v) + inter * _dot(q, ct.astype(BF16))
    nq = jnp.sum(q.astype(F32) * n_ref[...], axis=1, keepdims=True)
    den = jnp.sum(sd, axis=1, keepdims=True) + inter * nq
    hh = num / jnp.maximum(jnp.abs(den), jnp.exp(-(b_col + mx_col)))

    hn = _rms(hh, hg_ref[0])
    out_ref[0] = (hn * _sigmoid(o_ref[0].astype(F32))).astype(out_ref.dtype)

    w_col = jnp.exp(a_col - mx_last)
    decay = jnp.exp(m_prev - mx_last)
    kw = k.astype(F32) * w_col
    ct_ref[...] = decay * ct + _dot(kw.T.astype(BF16), v)
    n_ref[...] = decay * n_ref[...] + jnp.sum(kw, axis=0, keepdims=True)
    m_ref[...] = b_last + mx_last


def _mlstm_core(q, k, v, o, gates_col, gates_row, head_gain):
    b, s, _ = v.shape
    L = ML_CHUNK
    return pl.pallas_call(
        _mlstm_kernel,
        grid=(b, ML_HEADS, s // L),
        in_specs=[
            pl.BlockSpec((1, L, ML_QK_DIM), lambda bi, hi, ci: (bi, ci, hi)),
            pl.BlockSpec((1, L, ML_QK_DIM), lambda bi, hi, ci: (bi, ci, hi)),
            pl.BlockSpec((1, L, ML_V_DIM), lambda bi, hi, ci: (bi, ci, hi)),
            pl.BlockSpec((1, L, ML_V_DIM), lambda bi, hi, ci: (bi, ci, hi)),
            pl.BlockSpec((1, L, LANES), lambda bi, hi, ci: (bi, ci, 0)),
            pl.BlockSpec((1, 2 * ML_HEADS, L), lambda bi, hi, ci: (bi, 0, ci)),
            pl.BlockSpec((1, 1, ML_V_DIM), lambda bi, hi, ci: (hi, 0, 0)),
        ],
        out_specs=pl.BlockSpec((1, L, ML_V_DIM), lambda bi, hi, ci: (bi, ci, hi)),
        out_shape=jax.ShapeDtypeStruct(v.shape, BF16),
        scratch_shapes=[pltpu.VMEM((ML_QK_DIM, ML_V_DIM), F32),
                        pltpu.VMEM((1, ML_QK_DIM), F32),
                        pltpu.VMEM((1, 1), F32)],
        compiler_params=_params("parallel", "parallel", "arbitrary"),
        name="mlstm_core",
    )(q, k, v, o, gates_col, gates_row, head_gain)


def _sb_inproj_kernel(x_ref, g_ref, wq_ref, wk_ref, wv_ref, q_ref, k_ref, v_ref):
    xn = _rms(x_ref[...], g_ref[...]).astype(BF16)
    q_ref[...] = (_dot(xn, wq_ref[...]) * (LOG2_E * SB_HEAD_DIM ** -0.5)).astype(BF16)
    k_ref[...] = _dot(xn, wk_ref[...]).astype(BF16)
    v_ref[...] = _dot(xn, wv_ref[...]).astype(BF16)


def _sb_inproj(x, g, wq, wk, wv):
    m, d = x.shape
    tm = ROW_TILE
    row = lambda n: pl.BlockSpec((tm, n), lambda i: (i, 0))
    return pl.pallas_call(
        _sb_inproj_kernel,
        grid=(m // tm,),
        in_specs=[row(d), _const_spec((1, d)), _const_spec(wq.shape), _const_spec(wk.shape),
                  _const_spec(wv.shape)],
        out_specs=[row(d), row(d), row(d)],
        out_shape=[jax.ShapeDtypeStruct((m, d), BF16)] * 3,
        compiler_params=_params("parallel"),
        name="sb_inproj",
    )(x, g, wq, wk, wv)


def _sb_kernel(q_ref, k_ref, v_ref, o_ref):
    qi = pl.program_id(2)
    T = SB_TILE
    q = q_ref[0]
    lane = lax.broadcasted_iota(jnp.int32, q.shape, 1)
    first = lane < SB_HEAD_DIM
    zero = jnp.zeros_like(q)
    qq = jnp.concatenate([jnp.where(first, q, zero), jnp.where(first, zero, q)], axis=0)

    row = lax.broadcasted_iota(jnp.int32, (T, T), 0)
    col = lax.broadcasted_iota(jnp.int32, (T, T), 1)
    neg_suffix = jnp.where(row >= col, -1.0, 0.0).astype(BF16)
    strict1 = col < row
    strict = jnp.concatenate([strict1, strict1], axis=0)

    def tile(kb, vb, carry, diagonal):
        z = _dot_nt(qq, kb)
        sp = jnp.maximum(z, 0.0) + jnp.log(1.0 + jnp.exp2(-jnp.abs(z))) * LOG2_E
        if diagonal:
            sp = jnp.where(strict, sp, 0.0)
        cum = _dot(sp.astype(BF16), neg_suffix)
        a = jnp.exp2(z + cum + carry)
        if diagonal:
            a = jnp.where(strict, a, 0.0)
        return _dot(a.astype(BF16), vb), carry + cum[:, 0:1]

    start = pl.multiple_of(qi * T, T)
    kd = k_ref[0, pl.ds(start, T), :]
    vd = v_ref[0, pl.ds(start, T), :]
    acc, c = tile(kd, vd, jnp.zeros((2 * T, 1), F32), True)

    def full_tile(idx, acc, c):
        off = pl.multiple_of(idx * T, T)
        p, c = tile(k_ref[0, pl.ds(off, T), :], v_ref[0, pl.ds(off, T), :], c, False)
        return acc + p, c

    odd = qi % 2
    acc, c = lax.cond(odd == 1, lambda a, b: full_tile(qi - 1, a, b), lambda a, b: (a, b), acc, c)
    n_even = qi - odd

    def body(j, carry):
        acc, c = carry
        acc, c = full_tile(n_even - 1 - 2 * j, acc, c)
        return full_tile(n_even - 2 - 2 * j, acc, c)

    acc, _ = lax.fori_loop(0, n_even // 2, body, (acc, c))
    o_ref[0] = jnp.where(first, acc[:T], acc[T:]).astype(o_ref.dtype)


def _sb_core(q, k, v):
    b, s, d = q.shape
    T = SB_TILE
    return pl.pallas_call(
        _sb_kernel,
        grid=(b, d // LANES, s // T),
        in_specs=[pl.BlockSpec((1, T, LANES), lambda bi, hi, ti: (bi, ti, hi)),
                  pl.BlockSpec((1, s, LANES), lambda bi, hi, ti: (bi, 0, hi)),
                  pl.BlockSpec((1, s, LANES), lambda bi, hi, ti: (bi, 0, hi))],
        out_specs=pl.BlockSpec((1, T, LANES), lambda bi, hi, ti: (bi, ti, hi)),
        out_shape=jax.ShapeDtypeStruct(q.shape, BF16),
        compiler_params=_params("parallel", "parallel", "arbitrary"),
        name="sb_core",
    )(q, k, v)


def _outproj_kernel(a_ref, w_ref, g_ref, x_ref, o_ref):
    y = _dot(a_ref[...], w_ref[...])
    o_ref[...] = x_ref[...] + _rms(y, g_ref[...])


def _outproj(a, w, g, x):
    m, d = x.shape
    kdim = a.shape[1]
    tm = ROW_TILE
    return pl.pallas_call(
        _outproj_kernel,
        grid=(m // tm,),
        in_specs=[pl.BlockSpec((tm, kdim), lambda i: (i, 0)), _const_spec(w.shape), _const_spec((1, d)),
                  pl.BlockSpec((tm, d), lambda i: (i, 0))],
        out_specs=pl.BlockSpec((tm, d), lambda i: (i, 0)),
        out_shape=jax.ShapeDtypeStruct((m, d), F32),
        compiler_params=_params("parallel"),
        name="outproj",
    )(a, w, g, x)


def _memkv_kernel(mem_ref, g_ref, w_ref, o_ref):
    mn = _rms(mem_ref[...], g_ref[...]).astype(BF16)
    o_ref[0] = _dot(mn, w_ref[0]).astype(o_ref.dtype)


def _memkv(mem, g, w_kv):
    m, d = mem.shape
    depth, _, n = w_kv.shape
    tm = ROW_TILE
    return pl.pallas_call(
        _memkv_kernel,
        grid=(depth, m // tm),
        in_specs=[pl.BlockSpec((tm, d), lambda l, i: (i, 0)), _const_spec((1, d)),
                  pl.BlockSpec((1, d, n), lambda l, i: (l, 0, 0))],
        out_specs=pl.BlockSpec((1, tm, n), lambda l, i: (l, i, 0)),
        out_shape=jax.ShapeDtypeStruct((depth, m, n), BF16),
        compiler_params=_params("parallel", "parallel"),
        name="memkv",
    )(mem, g, w_kv)


def _xattn_kernel(x_ref, gpre_ref, gpost_ref, wq_ref, kv_ref, wo_ref, o_ref):
    x = x_ref[0]
    xn = _rms(x, gpre_ref[...]).astype(BF16)
    q = (_dot(xn, wq_ref[...]) * (XA_HEAD_DIM ** -0.5)).astype(BF16)
    d = XA_HEADS * XA_HEAD_DIM
    heads = []
    for h in range(XA_HEADS):
        lo, hi = h * XA_HEAD_DIM, (h + 1) * XA_HEAD_DIM
        s = _dot_nt(q[:, lo:hi], kv_ref[0, 0, :, lo:hi])
        p = jnp.exp(s - jnp.max(s, axis=-1, keepdims=True))
        denom = jnp.sum(p, axis=-1, keepdims=True)
        heads.append((_dot(p.astype(BF16), kv_ref[0, 0, :, d + lo:d + hi]) / denom).astype(BF16))
    attn = jnp.concatenate(heads, axis=-1)
    y = _dot(attn, wo_ref[...])
    o_ref[0] = x + _rms(y, gpost_ref[...])


def _xattn(x, gpre, gpost, wq, kv, layer, wo):
    b, s, d = x.shape
    n_mem = kv.shape[2]
    tm = ROW_TILE
    return pl.pallas_call(
        _xattn_kernel,
        grid=(b, s // tm),
        in_specs=[pl.BlockSpec((1, tm, d), lambda bi, i: (bi, i, 0)), _const_spec((1, d)),
                  _const_spec((1, d)), _const_spec(wq.shape),
                  pl.BlockSpec((1, 1, n_mem, 2 * d), lambda bi, i: (layer, bi, 0, 0)),
                  _const_spec(wo.shape)],
        out_specs=pl.BlockSpec((1, tm, d), lambda bi, i: (bi, i, 0)),
        out_shape=jax.ShapeDtypeStruct(x.shape, F32),
        compiler_params=_params("parallel", "parallel"),
        name="xattn",
    )(x, gpre, gpost, wq, kv, wo)


def _ffn_kernel(x_ref, gpre_ref, gpost_ref, wg_ref, wu_ref, wd_ref, o_ref):
    x = x_ref[...]
    xn = _rms(x, gpre_ref[...]).astype(BF16)
    d_ff = wg_ref.shape[1]
    y = jnp.zeros(x.shape, F32)
    for c0 in range(0, d_ff, FF_CHUNK):
        gate = _dot(xn, wg_ref[:, c0:c0 + FF_CHUNK])
        up = _dot(xn, wu_ref[:, c0:c0 + FF_CHUNK])
        hid = (gate * _sigmoid(gate) * up).astype(BF16)
        y = y + _dot(hid, wd_ref[c0:c0 + FF_CHUNK, :])
    o_ref[...] = x + _rms(y, gpost_ref[...])


def _ffn(x, gpre, gpost, wg, wu, wd):
    m, d = x.shape
    tm = ROW_TILE
    return pl.pallas_call(
        _ffn_kernel,
        grid=(m // tm,),
        in_specs=[pl.BlockSpec((tm, d), lambda i: (i, 0)), _const_spec((1, d)), _const_spec((1, d)),
                  _const_spec(wg.shape), _const_spec(wu.shape), _const_spec(wd.shape)],
        out_specs=pl.BlockSpec((tm, d), lambda i: (i, 0)),
        out_shape=jax.ShapeDtypeStruct((m, d), F32),
        compiler_params=_params("parallel"),
        name="ffn",
    )(x, gpre, gpost, wg, wu, wd)


def kernel(x, mem, mem_norm_gain, norm_gains, ml_w_in, ml_b_gate, ml_head_gain, ml_w_out, sb_w_qkv, sb_w_out,
           xa_w_q, xa_w_kv, xa_w_o, ffn_w_gate_up, ffn_w_down):
    b, s, d = x.shape
    depth = norm_gains.shape[0]
    n_mem = mem.shape[1]
    m = b * s
    d_ff = ffn_w_down.shape[1]
    qk = ML_HEADS * ML_QK_DIM
    n_gates = 2 * ML_HEADS
    gains = norm_gains.reshape(depth, norm_gains.shape[1], 1, d)

    kv = _memkv(mem.reshape(b * n_mem, d), mem_norm_gain.reshape(1, d), xa_w_kv.astype(BF16))
    kv = kv.reshape(depth, b, n_mem, 2 * d)

    xf = x.reshape(m, d)
    for layer in range(depth):
        g = gains[layer]
        j = layer // 2
        if layer % 2 == 0:
            w_in = ml_w_in[j]
            wg = jnp.pad(w_in[:, 2 * qk + 2 * d:], ((0, 0), (0, LANES - n_gates))).astype(BF16)
            bg = jnp.pad(ml_b_gate[j], (0, LANES - n_gates)).reshape(1, LANES)
            q, k, v, o, gates = _ml_inproj(
                xf, g[0], w_in[:, :qk].astype(BF16), w_in[:, qk:2 * qk].astype(BF16),
                w_in[:, 2 * qk:2 * qk + d].astype(BF16), w_in[:, 2 * qk + d:2 * qk + 2 * d].astype(BF16), wg, bg)
            gates = gates.reshape(b, s, LANES)
            gates_row = jnp.swapaxes(gates[:, :, :n_gates], 1, 2)
            mixed = _mlstm_core(q.reshape(b, s, qk), k.reshape(b, s, qk), v.reshape(b, s, d),
                                o.reshape(b, s, d), gates, gates_row,
                                ml_head_gain[j].reshape(ML_HEADS, 1, ML_V_DIM))
            xf = _outproj(mixed.reshape(m, d), ml_w_out[j].astype(BF16), g[1], xf)
        else:
            w = sb_w_qkv[j]
            q, k, v = _sb_inproj(xf, g[0], w[:, :d].astype(BF16), w[:, d:2 * d].astype(BF16),
                                 w[:, 2 * d:].astype(BF16))
            mixed = _sb_core(q.reshape(b, s, d), k.reshape(b, s, d), v.reshape(b, s, d))
            xf = _outproj(mixed.reshape(m, d), sb_w_out[j].astype(BF16), g[1], xf)
        xf = _xattn(xf.reshape(b, s, d), g[2], g[3], xa_w_q[layer].astype(BF16), kv, layer,
                    xa_w_o[layer].astype(BF16)).reshape(m, d)
        w_gu = ffn_w_gate_up[layer]
        xf = _ffn(xf, g[4], g[5], w_gu[:, :d_ff].astype(BF16), w_gu[:, d_ff:].astype(BF16),
                  ffn_w_down[layer].astype(BF16))
    return xf.reshape(b, s, d)
```

```python
import functools

import jax
import jax.numpy as jnp
from jax import lax
from jax.experimental import pallas as pl
from jax.experimental.pallas import tpu as pltpu

F32 = jnp.float32
BF16 = jnp.bfloat16

RMS_EPS = 1e-6
ML_HEADS = 4
ML_QK_DIM = 128
ML_V_DIM = 256
ML_GATE_CAP = 15.0
SB_HEADS = 16
SB_HEAD_DIM = 64
XA_HEADS = 4
XA_HEAD_DIM = 256

LANES = 128
ROW_TILE = 1024
FFN_ROW_TILE = 512
ML_CHUNK = 128
SB_TILE = 256
FF_CHUNK = 256
VMEM_LIMIT = 56 * 1024 * 1024
NEG_BIG = -1e30
LOG2_E = 1.4426950408889634


def _rms(x, g):
    ms = jnp.mean(x * x, axis=-1, keepdims=True)
    return x * lax.rsqrt(ms + RMS_EPS) * g


def _sigmoid(x):
    return 1.0 / (1.0 + jnp.exp(-x))


def _log_sigmoid(x):
    return jnp.minimum(x, 0.0) - jnp.log1p(jnp.exp(-jnp.abs(x)))


def _dot(a, b):
    return jnp.dot(a, b, preferred_element_type=F32)


def _dot_nt(a, b):
    return lax.dot_general(a, b, (((1,), (1,)), ((), ())), preferred_element_type=F32)


def _rows2(fn, x):
    half = x.shape[0] // 2
    return jnp.concatenate([fn(x[:half]), fn(x[half:])], axis=0)


def _params(*sem):
    return pltpu.CompilerParams(dimension_semantics=sem, vmem_limit_bytes=VMEM_LIMIT)


def _const_spec(shape):
    return pl.BlockSpec(shape, lambda *_: (0,) * len(shape), pipeline_mode=pl.Buffered(1))


def _ml_inproj_kernel(x_ref, g_ref, wq_ref, wk_ref, wv_ref, wo_ref, wg_ref, bg_ref,
                      q_ref, k_ref, v_ref, o_ref, gate_ref):
    xn = _rms(x_ref[...], g_ref[...]).astype(BF16)
    q_ref[...] = _dot(xn, wq_ref[...]).astype(BF16)
    k_ref[...] = (_dot(xn, wk_ref[...]) * (ML_QK_DIM ** -0.5)).astype(BF16)
    v_ref[...] = _dot(xn, wv_ref[...]).astype(BF16)
    o_ref[...] = _dot(xn, wo_ref[...]).astype(BF16)
    gates = _dot(xn, wg_ref[...]) + bg_ref[...]
    gates = ML_GATE_CAP * jnp.tanh(gates / ML_GATE_CAP)
    lane = lax.broadcasted_iota(jnp.int32, gates.shape, 1)
    gate_ref[...] = jnp.where(lane < ML_HEADS, gates, _log_sigmoid(gates))


def _ml_inproj(x, g, wq, wk, wv, wo, wg, bg):
    m, d = x.shape
    tm = ROW_TILE
    row = lambda n: pl.BlockSpec((tm, n), lambda i: (i, 0))
    return pl.pallas_call(
        _ml_inproj_kernel,
        grid=(m // tm,),
        in_specs=[row(d), _const_spec((1, d)), _const_spec(wq.shape), _const_spec(wk.shape),
                  _const_spec(wv.shape), _const_spec(wo.shape), _const_spec(wg.shape),
                  _const_spec((1, LANES))],
        out_specs=[row(wq.shape[1]), row(wk.shape[1]), row(wv.shape[1]), row(wo.shape[1]), row(LANES)],
        out_shape=[jax.ShapeDtypeStruct((m, wq.shape[1]), BF16),
                   jax.ShapeDtypeStruct((m, wk.shape[1]), BF16),
                   jax.ShapeDtypeStruct((m, wv.shape[1]), BF16),
                   jax.ShapeDtypeStruct((m, wo.shape[1]), BF16),
                   jax.ShapeDtypeStruct((m, LANES), F32)],
        compiler_params=_params("parallel"),
        name="ml_inproj",
    )(x, g, wq, wk, wv, wo, wg, bg)


def _mlstm_kernel(q_ref, k_ref, v_ref, o_ref, gc_ref, gr_ref, hg_ref, out_ref, ct_ref, n_ref, m_ref):
    L = ML_CHUNK

    @pl.when(pl.program_id(1) == 0)
    def _():
        ct_ref[...] = jnp.zeros_like(ct_ref)
        n_ref[...] = jnp.zeros_like(n_ref)
        m_ref[...] = jnp.zeros_like(m_ref)

    gc = gc_ref[0]
    gr = gr_ref[0]
    lane = lax.broadcasted_iota(jnp.int32, gc.shape, 1)
    t_idx = lax.broadcasted_iota(jnp.int32, (L, L), 0)
    s_idx = lax.broadcasted_iota(jnp.int32, (L, L), 1)
    causal = s_idx <= t_idx
    anti = t_idx <= s_idx

    for h in range(ML_HEADS):
        qk = slice(h * ML_QK_DIM, (h + 1) * ML_QK_DIM)
        vd = slice(h * ML_V_DIM, (h + 1) * ML_V_DIM)
        q = q_ref[0, :, qk]
        k = k_ref[0, :, qk]
        v = v_ref[0, :, vd]
        ig_col = jnp.sum(jnp.where(lane == h, gc, 0.0), axis=1, keepdims=True)
        lf_col = jnp.sum(jnp.where(lane == h + ML_HEADS, gc, 0.0), axis=1, keepdims=True)
        ig_row = gr[h:h + 1, :]
        lf_row = gr[h + ML_HEADS:h + ML_HEADS + 1, :]

        b_col = jnp.sum(jnp.where(causal, lf_row, 0.0), axis=1, keepdims=True)
        b_row = jnp.sum(jnp.where(anti, lf_col, 0.0), axis=0, keepdims=True)
        a_row = ig_row - b_row
        a_col = ig_col - b_col

        m_prev = m_ref[h]
        mx_col = jnp.maximum(jnp.max(jnp.where(causal, a_row, NEG_BIG), axis=1, keepdims=True), m_prev)
        mx_last = mx_col[L - 1:L, :]
        b_last = b_col[L - 1:L, :]

        dmat = jnp.where(causal, jnp.exp(jnp.minimum(a_row - mx_col, 0.0)), 0.0)
        sd = _dot_nt(q, k) * dmat
        inter = jnp.exp(m_prev - mx_col)
        ct = ct_ref[h]
        num = _dot(sd.astype(BF16), v) + inter * _dot(q, ct.astype(BF16))
        nq = jnp.sum(q.astype(F32) * n_ref[h], axis=1, keepdims=True)
        den = jnp.sum(sd, axis=1, keepdims=True) + inter * nq
        hh = num / jnp.maximum(jnp.abs(den), jnp.exp(-(b_col + mx_col)))

        hn = _rms(hh, hg_ref[h])
        out_ref[0, :, vd] = (hn * _sigmoid(o_ref[0, :, vd].astype(F32))).astype(out_ref.dtype)

        w_col = jnp.exp(a_col - mx_last)
        decay = jnp.exp(m_prev - mx_last)
        kw = k.astype(F32) * w_col
        ct_ref[h] = decay * ct + _dot(kw.T.astype(BF16), v)
        n_ref[h] = decay * n_ref[h] + jnp.sum(kw, axis=0, keepdims=True)
        m_ref[h] = b_last + mx_last


def _mlstm_core(q, k, v, o, gates_col, gates_row, head_gain):
    b, s, d = v.shape
    L = ML_CHUNK
    qk = ML_HEADS * ML_QK_DIM
    return pl.pallas_call(
        _mlstm_kernel,
        grid=(b, s // L),
        in_specs=[
            pl.BlockSpec((1, L, qk), lambda bi, ci: (bi, ci, 0)),
            pl.BlockSpec((1, L, qk), lambda bi, ci: (bi, ci, 0)),
            pl.BlockSpec((1, L, d), lambda bi, ci: (bi, ci, 0)),
            pl.BlockSpec((1, L, d), lambda bi, ci: (bi, ci, 0)),
            pl.BlockSpec((1, L, LANES), lambda bi, ci: (bi, ci, 0)),
            pl.BlockSpec((1, 2 * ML_HEADS, L), lambda bi, ci: (bi, 0, ci)),
            _const_spec((ML_HEADS, 1, ML_V_DIM)),
        ],
        out_specs=pl.BlockSpec((1, L, d), lambda bi, ci: (bi, ci, 0)),
        out_shape=jax.ShapeDtypeStruct(v.shape, BF16),
        scratch_shapes=[pltpu.VMEM((ML_HEADS, ML_QK_DIM, ML_V_DIM), F32),
                        pltpu.VMEM((ML_HEADS, 1, ML_QK_DIM), F32),
                        pltpu.VMEM((ML_HEADS, 1, 1), F32)],
        compiler_params=_params("parallel", "arbitrary"),
        name="mlstm_core",
    )(q, k, v, o, gates_col, gates_row, head_gain)


def _sb_inproj_kernel(x_ref, g_ref, wq_ref, wk_ref, wv_ref, q_ref, k_ref, v_ref):
    xn = _rms(x_ref[...], g_ref[...]).astype(BF16)
    q_ref[...] = (_dot(xn, wq_ref[...]) * (LOG2_E * SB_HEAD_DIM ** -0.5)).astype(BF16)
    k_ref[...] = _dot(xn, wk_ref[...]).astype(BF16)
    v_ref[...] = _dot(xn, wv_ref[...]).astype(BF16)


def _sb_inproj(x, g, wq, wk, wv):
    m, d = x.shape
    tm = ROW_TILE
    row = lambda n: pl.BlockSpec((tm, n), lambda i: (i, 0))
    return pl.pallas_call(
        _sb_inproj_kernel,
        grid=(m // tm,),
        in_specs=[row(d), _const_spec((1, d)), _const_spec(wq.shape), _const_spec(wk.shape),
                  _const_spec(wv.shape)],
        out_specs=[row(d), row(d), row(d)],
        out_shape=[jax.ShapeDtypeStruct((m, d), BF16)] * 3,
        compiler_params=_params("parallel"),
        name="sb_inproj",
    )(x, g, wq, wk, wv)


def _sb_kernel(q_ref, k_ref, v_ref, o_ref, z0_ref, z1_ref, sp0_ref, sp1_ref, acc_ref, c_ref):
    qi = pl.program_id(2)
    T = SB_TILE
    q = q_ref[0]
    lane = lax.broadcasted_iota(jnp.int32, q.shape, 1)
    first = lane < SB_HEAD_DIM
    zero = jnp.zeros_like(q)
    qq = jnp.concatenate([jnp.where(first, q, zero), jnp.where(first, zero, q)], axis=0)

    row = lax.broadcasted_iota(jnp.int32, (T, T), 0)
    col = lax.broadcasted_iota(jnp.int32, (T, T), 1)
    neg_suffix = jnp.where(row >= col, -1.0, 0.0).astype(BF16)
    strict1 = col < row
    strict = jnp.concatenate([strict1, strict1], axis=0)

    z_slots = (z0_ref, z1_ref)
    sp_slots = (sp0_ref, sp1_ref)

    def logits(idx, slot, diagonal=False):
        kb = k_ref[0, pl.ds(pl.multiple_of(idx * T, T), T), :]
        z = _rows2(lambda x: _dot_nt(x, kb), qq)
        if diagonal:
            z = jnp.where(strict, z, NEG_BIG)
        z_slots[slot][...] = z

    def weights(idx, slot):
        vb = v_ref[0, pl.ds(pl.multiple_of(idx * T, T), T), :]
        z = z_slots[slot][...]
        sp = jnp.maximum(z, 0.0) + jnp.log(1.0 + jnp.exp2(-jnp.abs(z))) * LOG2_E
        cum = _rows2(lambda x: _dot(x, neg_suffix), sp.astype(BF16))
        carry = c_ref[...]
        t = z + cum
        a = jnp.exp2(jnp.concatenate([t[:, :LANES] + carry, t[:, LANES:] + carry], axis=1))
        acc_ref[...] += _rows2(lambda x: _dot(x, vb), a.astype(BF16))
        c_ref[...] = carry + jnp.broadcast_to(cum[:, 0:1], carry.shape)

    acc_ref[...] = jnp.zeros_like(acc_ref)
    c_ref[...] = jnp.zeros_like(c_ref)
    odd = qi % 2

    @pl.when(odd == 0)
    def _():
        logits(qi, 0, diagonal=True)

    @pl.when(odd == 1)
    def _():
        logits(qi, 1, diagonal=True)
        logits(qi - 1, 0)
        weights(qi, 1)

    n_even = qi - odd

    def body(i, _):
        t = n_even - 2 * i
        logits(t - 1, 1)
        weights(t, 0)
        logits(t - 2, 0)
        weights(t - 1, 1)
        return 0

    lax.fori_loop(0, n_even // 2, body, 0)
    weights(0, 0)
    acc = acc_ref[...]
    o_ref[0] = jnp.where(first, acc[:T], acc[T:]).astype(o_ref.dtype)


def _sb_core(q, k, v):
    b, s, d = q.shape
    T = SB_TILE
    return pl.pallas_call(
        _sb_kernel,
        grid=(b, d // LANES, s // T),
        in_specs=[pl.BlockSpec((1, T, LANES), lambda bi, hi, ti: (bi, ti, hi)),
                  pl.BlockSpec((1, s, LANES), lambda bi, hi, ti: (bi, 0, hi)),
                  pl.BlockSpec((1, s, LANES), lambda bi, hi, ti: (bi, 0, hi))],
        out_specs=pl.BlockSpec((1, T, LANES), lambda bi, hi, ti: (bi, ti, hi)),
        out_shape=jax.ShapeDtypeStruct(q.shape, BF16),
        scratch_shapes=[pltpu.VMEM((2 * T, T), F32), pltpu.VMEM((2 * T, T), F32),
                        pltpu.VMEM((2 * T, T), BF16), pltpu.VMEM((2 * T, T), BF16),
                        pltpu.VMEM((2 * T, LANES), F32), pltpu.VMEM((2 * T, LANES), F32)],
        compiler_params=_params("parallel", "parallel", "arbitrary"),
        name="sb_core",
    )(q, k, v)


def _outproj_kernel(a_ref, w_ref, g_ref, x_ref, o_ref):
    y = _dot(a_ref[...], w_ref[...])
    o_ref[...] = x_ref[...] + _rms(y, g_ref[...])


def _outproj(a, w, g, x):
    m, d = x.shape
    kdim = a.shape[1]
    tm = ROW_TILE
    return pl.pallas_call(
        _outproj_kernel,
        grid=(m // tm,),
        in_specs=[pl.BlockSpec((tm, kdim), lambda i: (i, 0)), _const_spec(w.shape), _const_spec((1, d)),
                  pl.BlockSpec((tm, d), lambda i: (i, 0))],
        out_specs=pl.BlockSpec((tm, d), lambda i: (i, 0)),
        out_shape=jax.ShapeDtypeStruct((m, d), F32),
        compiler_params=_params("parallel"),
        name="outproj",
    )(a, w, g, x)


def _memkv_kernel(mem_ref, g_ref, w_ref, o_ref):
    mn = _rms(mem_ref[...], g_ref[...]).astype(BF16)
    o_ref[0] = _dot(mn, w_ref[0]).astype(o_ref.dtype)


def _memkv(mem, g, w_kv):
    m, d = mem.shape
    depth, _, n = w_kv.shape
    tm = ROW_TILE
    return pl.pallas_call(
        _memkv_kernel,
        grid=(depth, m // tm),
        in_specs=[pl.BlockSpec((tm, d), lambda l, i: (i, 0)), _const_spec((1, d)),
                  pl.BlockSpec((1, d, n), lambda l, i: (l, 0, 0))],
        out_specs=pl.BlockSpec((1, tm, n), lambda l, i: (l, i, 0)),
        out_shape=jax.ShapeDtypeStruct((depth, m, n), BF16),
        compiler_params=_params("parallel", "parallel"),
        name="memkv",
    )(mem, g, w_kv)


def _xattn_kernel(x_ref, gpre_ref, gpost_ref, wq_ref, kv_ref, wo_ref, o_ref):
    x = x_ref[0]
    xn = _rms(x, gpre_ref[...]).astype(BF16)
    q = (_dot(xn, wq_ref[...]) * (XA_HEAD_DIM ** -0.5)).astype(BF16)
    d = XA_HEADS * XA_HEAD_DIM
    heads = []
    for h in range(XA_HEADS):
        lo, hi = h * XA_HEAD_DIM, (h + 1) * XA_HEAD_DIM
        s = _dot_nt(q[:, lo:hi], kv_ref[0, 0, :, lo:hi])
        p = jnp.exp(s - jnp.max(s, axis=-1, keepdims=True))
        denom = jnp.sum(p, axis=-1, keepdims=True)
        heads.append((_dot(p.astype(BF16), kv_ref[0, 0, :, d + lo:d + hi]) / denom).astype(BF16))
    attn = jnp.concatenate(heads, axis=-1)
    y = _dot(attn, wo_ref[...])
    o_ref[0] = x + _rms(y, gpost_ref[...])


def _xattn(x, gpre, gpost, wq, kv, layer, wo):
    b, s, d = x.shape
    n_mem = kv.shape[2]
    tm = ROW_TILE
    return pl.pallas_call(
        _xattn_kernel,
        grid=(b, s // tm),
        in_specs=[pl.BlockSpec((1, tm, d), lambda bi, i: (bi, i, 0)), _const_spec((1, d)),
                  _const_spec((1, d)), _const_spec(wq.shape),
                  pl.BlockSpec((1, 1, n_mem, 2 * d), lambda bi, i: (layer, bi, 0, 0)),
                  _const_spec(wo.shape)],
        out_specs=pl.BlockSpec((1, tm, d), lambda bi, i: (bi, i, 0)),
        out_shape=jax.ShapeDtypeStruct(x.shape, F32),
        compiler_params=_params("parallel", "parallel"),
        name="xattn",
    )(x, gpre, gpost, wq, kv, wo)


def _ffn_kernel(x_ref, gpre_ref, gpost_ref, wg_ref, wu_ref, wd_ref, o_ref):
    x = x_ref[...]
    xn = _rms(x, gpre_ref[...]).astype(BF16)
    d_ff = wg_ref.shape[1]
    y = jnp.zeros(x.shape, F32)
    for c0 in range(0, d_ff, FF_CHUNK):
        gate = _dot(xn, wg_ref[:, c0:c0 + FF_CHUNK])
        up = _dot(xn, wu_ref[:, c0:c0 + FF_CHUNK])
        hid = (gate * _sigmoid(gate) * up).astype(BF16)
        y = y + _dot(hid, wd_ref[c0:c0 + FF_CHUNK, :])
    o_ref[...] = x + _rms(y, gpost_ref[...])


def _ffn(x, gpre, gpost, wg, wu, wd):
    m, d = x.shape
    tm = FFN_ROW_TILE
    return pl.pallas_call(
        _ffn_kernel,
        grid=(m // tm,),
        in_specs=[pl.BlockSpec((tm, d), lambda i: (i, 0)), _const_spec((1, d)), _const_spec((1, d)),
                  _const_spec(wg.shape), _const_spec(wu.shape), _const_spec(wd.shape)],
        out_specs=pl.BlockSpec((tm, d), lambda i: (i, 0)),
        out_shape=jax.ShapeDtypeStruct((m, d), F32),
        compiler_params=_params("parallel"),
        name="ffn",
    )(x, gpre, gpost, wg, wu, wd)


def kernel(x, mem, mem_norm_gain, norm_gains, ml_w_in, ml_b_gate, ml_head_gain, ml_w_out, sb_w_qkv, sb_w_out,
           xa_w_q, xa_w_kv, xa_w_o, ffn_w_gate_up, ffn_w_down):
    b, s, d = x.shape
    depth = norm_gains.shape[0]
    n_mem = mem.shape[1]
    m = b * s
    d_ff = ffn_w_down.shape[1]
    qk = ML_HEADS * ML_QK_DIM
    n_gates = 2 * ML_HEADS
    gains = norm_gains.reshape(depth, norm_gains.shape[1], 1, d)

    kv = _memkv(mem.reshape(b * n_mem, d), mem_norm_gain.reshape(1, d), xa_w_kv.astype(BF16))
    kv = kv.reshape(depth, b, n_mem, 2 * d)

    xf = x.reshape(m, d)
    for layer in range(depth):
        g = gains[layer]
        j = layer // 2
        if layer % 2 == 0:
            w_in = ml_w_in[j]
            wg = jnp.pad(w_in[:, 2 * qk + 2 * d:], ((0, 0), (0, LANES - n_gates))).astype(BF16)
            bg = jnp.pad(ml_b_gate[j], (0, LANES - n_gates)).reshape(1, LANES)
            q, k, v, o, gates = _ml_inproj(
                xf, g[0], w_in[:, :qk].astype(BF16), w_in[:, qk:2 * qk].astype(BF16),
                w_in[:, 2 * qk:2 * qk + d].astype(BF16), w_in[:, 2 * qk + d:2 * qk + 2 * d].astype(BF16), wg, bg)
            gates = gates.reshape(b, s, LANES)
            gates_row = jnp.swapaxes(gates[:, :, :n_gates], 1, 2)
            mixed = _mlstm_core(q.reshape(b, s, qk), k.reshape(b, s, qk), v.reshape(b, s, d),
                                o.reshape(b, s, d), gates, gates_row,
                                ml_head_gain[j].reshape(ML_HEADS, 1, ML_V_DIM))
            xf = _outproj(mixed.reshape(m, d), ml_w_out[j].astype(BF16), g[1], xf)
        else:
            w = sb_w_qkv[j]
            q, k, v = _sb_inproj(xf, g[0], w[:, :d].astype(BF16), w[:, d:2 * d].astype(BF16),
                                 w[:, 2 * d:].astype(BF16))
            mixed = _sb_core(q.reshape(b, s, d), k.reshape(b, s, d), v.reshape(b, s, d))
            xf = _outproj(mixed.reshape(m, d), sb_w_out[j].astype(BF16), g[1], xf)
        xf = _xattn(xf.reshape(b, s, d), g[2], g[3], xa_w_q[layer].astype(BF16), kv, layer,
                    xa_w_o[layer].astype(BF16)).reshape(m, d)
        w_gu = ffn_w_gate_up[layer]
        xf = _ffn(xf, g[4], g[5], w_gu[:, :d_ff].astype(BF16), w_gu[:, d_ff:].astype(BF16),
                  ffn_w_down[layer].astype(BF16))
    return xf.reshape(b, s, d)
```

```python
import functools

import jax
import jax.numpy as jnp
from jax import lax
from jax.experimental import pallas as pl
from jax.experimental.pallas import tpu as pltpu

F32 = jnp.float32
BF16 = jnp.bfloat16

RMS_EPS = 1e-6
ML_HEADS = 4
ML_QK_DIM = 128
ML_V_DIM = 256
ML_GATE_CAP = 15.0
SB_HEADS = 16
SB_HEAD_DIM = 64
XA_HEADS = 4
XA_HEAD_DIM = 256

LANES = 128
ROW_TILE = 1024
FFN_ROW_TILE = 512
ML_CHUNK = 256
SB_TILE = 256
SB_CHAINS = 4
FF_CHUNK = 256
VMEM_LIMIT = 56 * 1024 * 1024
NEG_BIG = -1e30
LOG2_E = 1.4426950408889634


def _rms(x, g):
    ms = jnp.mean(x * x, axis=-1, keepdims=True)
    return x * lax.rsqrt(ms + RMS_EPS) * g


def _sigmoid(x):
    return 1.0 / (1.0 + jnp.exp(-x))


def _log_sigmoid(x):
    return jnp.minimum(x, 0.0) - jnp.log1p(jnp.exp(-jnp.abs(x)))


def _dot(a, b):
    return jnp.dot(a, b, preferred_element_type=F32)


def _dot_nt(a, b):
    return lax.dot_general(a, b, (((1,), (1,)), ((), ())), preferred_element_type=F32)


def _rows2(fn, x):
    half = x.shape[0] // 2
    return jnp.concatenate([fn(x[:half]), fn(x[half:])], axis=0)


def _params(*sem):
    return pltpu.CompilerParams(dimension_semantics=sem, vmem_limit_bytes=VMEM_LIMIT)


def _const_spec(shape):
    return pl.BlockSpec(shape, lambda *_: (0,) * len(shape), pipeline_mode=pl.Buffered(1))


def _ml_inproj_kernel(x_ref, g_ref, wq_ref, wk_ref, wv_ref, wo_ref, wg_ref, bg_ref,
                      q_ref, k_ref, v_ref, o_ref, gate_ref):
    xn = _rms(x_ref[...], g_ref[...]).astype(BF16)
    q_ref[...] = _dot(xn, wq_ref[...]).astype(BF16)
    k_ref[...] = (_dot(xn, wk_ref[...]) * (ML_QK_DIM ** -0.5)).astype(BF16)
    v_ref[...] = _dot(xn, wv_ref[...]).astype(BF16)
    o_ref[...] = _dot(xn, wo_ref[...]).astype(BF16)
    gates = _dot(xn, wg_ref[...]) + bg_ref[...]
    gates = ML_GATE_CAP * jnp.tanh(gates / ML_GATE_CAP)
    lane = lax.broadcasted_iota(jnp.int32, gates.shape, 1)
    gate_ref[...] = jnp.where(lane < ML_HEADS, gates, _log_sigmoid(gates))


def _ml_inproj(x, g, wq, wk, wv, wo, wg, bg):
    m, d = x.shape
    tm = ROW_TILE
    row = lambda n: pl.BlockSpec((tm, n), lambda i: (i, 0))
    return pl.pallas_call(
        _ml_inproj_kernel,
        grid=(m // tm,),
        in_specs=[row(d), _const_spec((1, d)), _const_spec(wq.shape), _const_spec(wk.shape),
                  _const_spec(wv.shape), _const_spec(wo.shape), _const_spec(wg.shape),
                  _const_spec((1, LANES))],
        out_specs=[row(wq.shape[1]), row(wk.shape[1]), row(wv.shape[1]), row(wo.shape[1]), row(LANES)],
        out_shape=[jax.ShapeDtypeStruct((m, wq.shape[1]), BF16),
                   jax.ShapeDtypeStruct((m, wk.shape[1]), BF16),
                   jax.ShapeDtypeStruct((m, wv.shape[1]), BF16),
                   jax.ShapeDtypeStruct((m, wo.shape[1]), BF16),
                   jax.ShapeDtypeStruct((m, LANES), F32)],
        compiler_params=_params("parallel"),
        name="ml_inproj",
    )(x, g, wq, wk, wv, wo, wg, bg)


def _mlstm_kernel(q_ref, k_ref, v_ref, o_ref, gc_ref, gr_ref, hg_ref, out_ref, ct_ref, n_ref, m_ref):
    L = ML_CHUNK

    @pl.when(pl.program_id(1) == 0)
    def _():
        ct_ref[...] = jnp.zeros_like(ct_ref)
        n_ref[...] = jnp.zeros_like(n_ref)
        m_ref[...] = jnp.zeros_like(m_ref)

    gc = gc_ref[0]
    gr = gr_ref[0]
    lane = lax.broadcasted_iota(jnp.int32, gc.shape, 1)
    t_idx = lax.broadcasted_iota(jnp.int32, (L, L), 0)
    s_idx = lax.broadcasted_iota(jnp.int32, (L, L), 1)
    causal = s_idx <= t_idx
    anti = t_idx <= s_idx

    for h in range(ML_HEADS):
        qk = slice(h * ML_QK_DIM, (h + 1) * ML_QK_DIM)
        vd = slice(h * ML_V_DIM, (h + 1) * ML_V_DIM)
        q = q_ref[0, :, qk]
        k = k_ref[0, :, qk]
        v = v_ref[0, :, vd]
        ig_col = jnp.sum(jnp.where(lane == h, gc, 0.0), axis=1, keepdims=True)
        lf_col = jnp.sum(jnp.where(lane == h + ML_HEADS, gc, 0.0), axis=1, keepdims=True)
        ig_row = gr[h:h + 1, :]
        lf_row = gr[h + ML_HEADS:h + ML_HEADS + 1, :]

        b_col = jnp.sum(jnp.where(causal, lf_row, 0.0), axis=1, keepdims=True)
        b_row = jnp.sum(jnp.where(anti, lf_col, 0.0), axis=0, keepdims=True)
        a_row = ig_row - b_row
        a_col = ig_col - b_col

        m_prev = m_ref[h]
        mx_col = jnp.maximum(jnp.max(jnp.where(causal, a_row, NEG_BIG), axis=1, keepdims=True), m_prev)
        mx_last = mx_col[L - 1:L, :]
        b_last = b_col[L - 1:L, :]

        dmat = jnp.where(causal, jnp.exp(jnp.minimum(a_row - mx_col, 0.0)), 0.0)
        sd = _dot_nt(q, k) * dmat
        inter = jnp.exp(m_prev - mx_col)
        ct = ct_ref[h]
        num = _dot(sd.astype(BF16), v) + inter * _dot(q, ct.astype(BF16))
        nq = jnp.sum(q.astype(F32) * n_ref[h], axis=1, keepdims=True)
        den = jnp.sum(sd, axis=1, keepdims=True) + inter * nq
        hh = num / jnp.maximum(jnp.abs(den), jnp.exp(-(b_col + mx_col)))

        hn = _rms(hh, hg_ref[h])
        out_ref[0, :, vd] = (hn * _sigmoid(o_ref[0, :, vd].astype(F32))).astype(out_ref.dtype)

        w_col = jnp.exp(a_col - mx_last)
        decay = jnp.exp(m_prev - mx_last)
        kw = k.astype(F32) * w_col
        ct_ref[h] = decay * ct + _dot(kw.T.astype(BF16), v)
        n_ref[h] = decay * n_ref[h] + jnp.sum(kw, axis=0, keepdims=True)
        m_ref[h] = b_last + mx_last


def _mlstm_core(q, k, v, o, gates_col, gates_row, head_gain):
    b, s, d = v.shape
    L = ML_CHUNK
    qk = ML_HEADS * ML_QK_DIM
    return pl.pallas_call(
        _mlstm_kernel,
        grid=(b, s // L),
        in_specs=[
            pl.BlockSpec((1, L, qk), lambda bi, ci: (bi, ci, 0)),
            pl.BlockSpec((1, L, qk), lambda bi, ci: (bi, ci, 0)),
            pl.BlockSpec((1, L, d), lambda bi, ci: (bi, ci, 0)),
            pl.BlockSpec((1, L, d), lambda bi, ci: (bi, ci, 0)),
            pl.BlockSpec((1, L, LANES), lambda bi, ci: (bi, ci, 0)),
            pl.BlockSpec((1, 2 * ML_HEADS, L), lambda bi, ci: (bi, 0, ci)),
            _const_spec((ML_HEADS, 1, ML_V_DIM)),
        ],
        out_specs=pl.BlockSpec((1, L, d), lambda bi, ci: (bi, ci, 0)),
        out_shape=jax.ShapeDtypeStruct(v.shape, BF16),
        scratch_shapes=[pltpu.VMEM((ML_HEADS, ML_QK_DIM, ML_V_DIM), F32),
                        pltpu.VMEM((ML_HEADS, 1, ML_QK_DIM), F32),
                        pltpu.VMEM((ML_HEADS, 1, 1), F32)],
        compiler_params=_params("parallel", "arbitrary"),
        name="mlstm_core",
    )(q, k, v, o, gates_col, gates_row, head_gain)


def _sb_inproj_kernel(x_ref, g_ref, wq_ref, wk_ref, wv_ref, q_ref, k_ref, v_ref):
    xn = _rms(x_ref[...], g_ref[...]).astype(BF16)
    q_ref[...] = (_dot(xn, wq_ref[...]) * (LOG2_E * SB_HEAD_DIM ** -0.5)).astype(BF16)
    k_ref[...] = _dot(xn, wk_ref[...]).astype(BF16)
    v_ref[...] = _dot(xn, wv_ref[...]).astype(BF16)


def _sb_inproj(x, g, wq, wk, wv):
    m, d = x.shape
    tm = ROW_TILE
    row = lambda n: pl.BlockSpec((tm, n), lambda i: (i, 0))
    return pl.pallas_call(
        _sb_inproj_kernel,
        grid=(m // tm,),
        in_specs=[row(d), _const_spec((1, d)), _const_spec(wq.shape), _const_spec(wk.shape),
                  _const_spec(wv.shape)],
        out_specs=[row(d), row(d), row(d)],
        out_shape=[jax.ShapeDtypeStruct((m, d), BF16)] * 3,
        compiler_params=_params("parallel"),
        name="sb_inproj",
    )(x, g, wq, wk, wv)


def _sb_kernel(q_ref, k_ref, v_ref, o_ref, z_ref, acc_ref, c_ref):
    qi = pl.program_id(2)
    T = SB_TILE
    lane = lax.broadcasted_iota(jnp.int32, (T, LANES), 1)
    first = lane < SB_HEAD_DIM
    row = lax.broadcasted_iota(jnp.int32, (T, T), 0)
    col = lax.broadcasted_iota(jnp.int32, (T, T), 1)
    neg_suffix = jnp.where(row >= col, -1.0, 0.0).astype(BF16)
    strict1 = col < row
    strict = jnp.concatenate([strict1, strict1], axis=0)

    cols = [slice(ch * LANES, (ch + 1) * LANES) for ch in range(SB_CHAINS)]
    qq = []
    for ch in range(SB_CHAINS):
        q = q_ref[0, :, cols[ch]]
        zero = jnp.zeros_like(q)
        qq.append(jnp.concatenate([jnp.where(first, q, zero), jnp.where(first, zero, q)], axis=0))

    def logits(idx, slot, diagonal=False):
        for ch in range(SB_CHAINS):
            kb = k_ref[0, pl.ds(pl.multiple_of(idx * T, T), T), cols[ch]]
            z = _rows2(lambda x: _dot_nt(x, kb), qq[ch])
            if diagonal:
                z = jnp.where(strict, z, NEG_BIG)
            z_ref[ch, slot] = z

    def weights(idx, slot):
        for ch in range(SB_CHAINS):
            vb = v_ref[0, pl.ds(pl.multiple_of(idx * T, T), T), cols[ch]]
            z = z_ref[ch, slot]
            sp = jnp.maximum(z, 0.0) + jnp.log2(1.0 + jnp.exp2(-jnp.abs(z)))
            cum = _rows2(lambda x: _dot(x, neg_suffix), sp.astype(BF16))
            carry = c_ref[ch]
            t = z + cum
            a = jnp.exp2(jnp.concatenate([t[:, :LANES] + carry, t[:, LANES:] + carry], axis=1))
            acc_ref[ch] += _rows2(lambda x: _dot(x, vb), a.astype(BF16))
            c_ref[ch] = carry + jnp.broadcast_to(cum[:, 0:1], carry.shape)

    acc_ref[...] = jnp.zeros_like(acc_ref)
    c_ref[...] = jnp.zeros_like(c_ref)
    odd = qi % 2

    @pl.when(odd == 0)
    def _():
        logits(qi, 0, diagonal=True)

    @pl.when(odd == 1)
    def _():
        logits(qi, 1, diagonal=True)
        logits(qi - 1, 0)
        weights(qi, 1)

    n_even = qi - odd

    def body(i, _):
        t = n_even - 2 * i
        logits(t - 1, 1)
        weights(t, 0)
        logits(t - 2, 0)
        weights(t - 1, 1)
        return 0

    lax.fori_loop(0, n_even // 2, body, 0)
    weights(0, 0)
    for ch in range(SB_CHAINS):
        acc = acc_ref[ch]
        o_ref[0, :, cols[ch]] = jnp.where(first, acc[:T], acc[T:]).astype(o_ref.dtype)


def _sb_core(q, k, v):
    b, s, d = q.shape
    T = SB_TILE
    w = SB_CHAINS * LANES
    return pl.pallas_call(
        _sb_kernel,
        grid=(b, d // w, s // T),
        in_specs=[pl.BlockSpec((1, T, w), lambda bi, hi, ti: (bi, ti, hi)),
                  pl.BlockSpec((1, s, w), lambda bi, hi, ti: (bi, 0, hi)),
                  pl.BlockSpec((1, s, w), lambda bi, hi, ti: (bi, 0, hi))],
        out_specs=pl.BlockSpec((1, T, w), lambda bi, hi, ti: (bi, ti, hi)),
        out_shape=jax.ShapeDtypeStruct(q.shape, BF16),
        scratch_shapes=[pltpu.VMEM((SB_CHAINS, 2, 2 * T, T), F32),
                        pltpu.VMEM((SB_CHAINS, 2 * T, LANES), F32),
                        pltpu.VMEM((SB_CHAINS, 2 * T, LANES), F32)],
        compiler_params=_params("parallel", "parallel", "arbitrary"),
        name="sb_core",
    )(q, k, v)


def _outproj_kernel(a_ref, w_ref, g_ref, x_ref, o_ref):
    y = _dot(a_ref[...], w_ref[...])
    o_ref[...] = x_ref[...] + _rms(y, g_ref[...])


def _outproj(a, w, g, x):
    m, d = x.shape
    kdim = a.shape[1]
    tm = ROW_TILE
    return pl.pallas_call(
        _outproj_kernel,
        grid=(m // tm,),
        in_specs=[pl.BlockSpec((tm, kdim), lambda i: (i, 0)), _const_spec(w.shape), _const_spec((1, d)),
                  pl.BlockSpec((tm, d), lambda i: (i, 0))],
        out_specs=pl.BlockSpec((tm, d), lambda i: (i, 0)),
        out_shape=jax.ShapeDtypeStruct((m, d), F32),
        compiler_params=_params("parallel"),
        name="outproj",
    )(a, w, g, x)


def _memkv_kernel(mem_ref, g_ref, w_ref, o_ref):
    mn = _rms(mem_ref[...], g_ref[...]).astype(BF16)
    o_ref[0] = _dot(mn, w_ref[0]).astype(o_ref.dtype)


def _memkv(mem, g, w_kv):
    m, d = mem.shape
    depth, _, n = w_kv.shape
    tm = ROW_TILE
    return pl.pallas_call(
        _memkv_kernel,
        grid=(depth, m // tm),
        in_specs=[pl.BlockSpec((tm, d), lambda l, i: (i, 0)), _const_spec((1, d)),
                  pl.BlockSpec((1, d, n), lambda l, i: (l, 0, 0))],
        out_specs=pl.BlockSpec((1, tm, n), lambda l, i: (l, i, 0)),
        out_shape=jax.ShapeDtypeStruct((depth, m, n), BF16),
        compiler_params=_params("parallel", "parallel"),
        name="memkv",
    )(mem, g, w_kv)


def _xattn_kernel(x_ref, gpre_ref, gpost_ref, wq_ref, kv_ref, wo_ref, o_ref):
    x = x_ref[0]
    xn = _rms(x, gpre_ref[...]).astype(BF16)
    q = (_dot(xn, wq_ref[...]) * (XA_HEAD_DIM ** -0.5)).astype(BF16)
    d = XA_HEADS * XA_HEAD_DIM
    heads = []
    for h in range(XA_HEADS):
        lo, hi = h * XA_HEAD_DIM, (h + 1) * XA_HEAD_DIM
        s = _dot_nt(q[:, lo:hi], kv_ref[0, 0, :, lo:hi])
        p = jnp.exp(s - jnp.max(s, axis=-1, keepdims=True))
        denom = jnp.sum(p, axis=-1, keepdims=True)
        heads.append((_dot(p.astype(BF16), kv_ref[0, 0, :, d + lo:d + hi]) / denom).astype(BF16))
    attn = jnp.concatenate(heads, axis=-1)
    y = _dot(attn, wo_ref[...])
    o_ref[0] = x + _rms(y, gpost_ref[...])


def _xattn(x, gpre, gpost, wq, kv, layer, wo):
    b, s, d = x.shape
    n_mem = kv.shape[2]
    tm = ROW_TILE
    return pl.pallas_call(
        _xattn_kernel,
        grid=(b, s // tm),
        in_specs=[pl.BlockSpec((1, tm, d), lambda bi, i: (bi, i, 0)), _const_spec((1, d)),
                  _const_spec((1, d)), _const_spec(wq.shape),
                  pl.BlockSpec((1, 1, n_mem, 2 * d), lambda bi, i: (layer, bi, 0, 0)),
                  _const_spec(wo.shape)],
        out_specs=pl.BlockSpec((1, tm, d), lambda bi, i: (bi, i, 0)),
        out_shape=jax.ShapeDtypeStruct(x.shape, F32),
        compiler_params=_params("parallel", "parallel"),
        name="xattn",
    )(x, gpre, gpost, wq, kv, wo)


def _ffn_kernel(x_ref, gpre_ref, gpost_ref, wg_ref, wu_ref, wd_ref, o_ref):
    x = x_ref[...]
    xn = _rms(x, gpre_ref[...]).astype(BF16)
    d_ff = wg_ref.shape[1]
    y = jnp.zeros(x.shape, F32)
    for c0 in range(0, d_ff, FF_CHUNK):
        gate = _dot(xn, wg_ref[:, c0:c0 + FF_CHUNK])
        up = _dot(xn, wu_ref[:, c0:c0 + FF_CHUNK])
        hid = (gate * _sigmoid(gate) * up).astype(BF16)
        y = y + _dot(hid, wd_ref[c0:c0 + FF_CHUNK, :])
    o_ref[...] = x + _rms(y, gpost_ref[...])


def _ffn(x, gpre, gpost, wg, wu, wd):
    m, d = x.shape
    tm = FFN_ROW_TILE
    return pl.pallas_call(
        _ffn_kernel,
        grid=(m // tm,),
        in_specs=[pl.BlockSpec((tm, d), lambda i: (i, 0)), _const_spec((1, d)), _const_spec((1, d)),
                  _const_spec(wg.shape), _const_spec(wu.shape), _const_spec(wd.shape)],
        out_specs=pl.BlockSpec((tm, d), lambda i: (i, 0)),
        out_shape=jax.ShapeDtypeStruct((m, d), F32),
        compiler_params=_params("parallel"),
        name="ffn",
    )(x, gpre, gpost, wg, wu, wd)


def kernel(x, mem, mem_norm_gain, norm_gains, ml_w_in, ml_b_gate, ml_head_gain, ml_w_out, sb_w_qkv, sb_w_out,
           xa_w_q, xa_w_kv, xa_w_o, ffn_w_gate_up, ffn_w_down):
    b, s, d = x.shape
    depth = norm_gains.shape[0]
    n_mem = mem.shape[1]
    m = b * s
    d_ff = ffn_w_down.shape[1]
    qk = ML_HEADS * ML_QK_DIM
    n_gates = 2 * ML_HEADS
    gains = norm_gains.reshape(depth, norm_gains.shape[1], 1, d)

    kv = _memkv(mem.reshape(b * n_mem, d), mem_norm_gain.reshape(1, d), xa_w_kv.astype(BF16))
    kv = kv.reshape(depth, b, n_mem, 2 * d)

    xf = x.reshape(m, d)
    for layer in range(depth):
        g = gains[layer]
        j = layer // 2
        if layer % 2 == 0:
            w_in = ml_w_in[j]
            wg = jnp.pad(w_in[:, 2 * qk + 2 * d:], ((0, 0), (0, LANES - n_gates))).astype(BF16)
            bg = jnp.pad(ml_b_gate[j], (0, LANES - n_gates)).reshape(1, LANES)
            q, k, v, o, gates = _ml_inproj(
                xf, g[0], w_in[:, :qk].astype(BF16), w_in[:, qk:2 * qk].astype(BF16),
                w_in[:, 2 * qk:2 * qk + d].astype(BF16), w_in[:, 2 * qk + d:2 * qk + 2 * d].astype(BF16), wg, bg)
            gates = gates.reshape(b, s, LANES)
            gates_row = jnp.swapaxes(gates[:, :, :n_gates], 1, 2)
            mixed = _mlstm_core(q.reshape(b, s, qk), k.reshape(b, s, qk), v.reshape(b, s, d),
                                o.reshape(b, s, d), gates, gates_row,
                                ml_head_gain[j].reshape(ML_HEADS, 1, ML_V_DIM))
            xf = _outproj(mixed.reshape(m, d), ml_w_out[j].astype(BF16), g[1], xf)
        else:
            w = sb_w_qkv[j]
            q, k, v = _sb_inproj(xf, g[0], w[:, :d].astype(BF16), w[:, d:2 * d].astype(BF16),
                                 w[:, 2 * d:].astype(BF16))
            mixed = _sb_core(q.reshape(b, s, d), k.reshape(b, s, d), v.reshape(b, s, d))
            xf = _outproj(mixed.reshape(m, d), sb_w_out[j].astype(BF16), g[1], xf)
        xf = _xattn(xf.reshape(b, s, d), g[2], g[3], xa_w_q[layer].astype(BF16), kv, layer,
                    xa_w_o[layer].astype(BF16)).reshape(m, d)
        w_gu = ffn_w_gate_up[layer]
        xf = _ffn(xf, g[4], g[5], w_gu[:, :d_ff].astype(BF16), w_gu[:, d_ff:].astype(BF16),
                  ffn_w_down[layer].astype(BF16))
    return xf.reshape(b, s, d)
```

```python
import functools

import jax
import jax.numpy as jnp
from jax import lax
from jax.experimental import pallas as pl
from jax.experimental.pallas import tpu as pltpu

F32 = jnp.float32
BF16 = jnp.bfloat16

RMS_EPS = 1e-6
ML_HEADS = 4
ML_QK_DIM = 128
ML_V_DIM = 256
ML_GATE_CAP = 15.0
SB_HEADS = 16
SB_HEAD_DIM = 64
XA_HEADS = 4
XA_HEAD_DIM = 256

LANES = 128
ROW_TILE = 1024
FFN_ROW_TILE = 512
ML_CHUNK = 256
SB_TILE = 256
SB_CHAINS = 4
FF_CHUNK = 256
VMEM_LIMIT = 56 * 1024 * 1024
NEG_BIG = -1e30
LOG2_E = 1.4426950408889634


def _rms(x, g):
    ms = jnp.mean(x * x, axis=-1, keepdims=True)
    return x * lax.rsqrt(ms + RMS_EPS) * g


def _sigmoid(x):
    return 1.0 / (1.0 + jnp.exp(-x))


def _log_sigmoid(x):
    return jnp.minimum(x, 0.0) - jnp.log1p(jnp.exp(-jnp.abs(x)))


def _dot(a, b):
    return jnp.dot(a, b, preferred_element_type=F32)


def _dot_nt(a, b):
    return lax.dot_general(a, b, (((1,), (1,)), ((), ())), preferred_element_type=F32)


def _rows2(fn, x):
    half = x.shape[0] // 2
    return jnp.concatenate([fn(x[:half]), fn(x[half:])], axis=0)


def _params(*sem):
    return pltpu.CompilerParams(dimension_semantics=sem, vmem_limit_bytes=VMEM_LIMIT)


def _const_spec(shape):
    return pl.BlockSpec(shape, lambda *_: (0,) * len(shape), pipeline_mode=pl.Buffered(1))


def _layer_spec(w, layer):
    return pl.BlockSpec((1,) + w.shape[1:], lambda *_: (layer, 0, 0), pipeline_mode=pl.Buffered(1))


def _w(ref, rows=slice(None), cols=slice(None)):
    return ref[0, rows, cols].astype(BF16)


def _ml_inproj_kernel(x_ref, g_ref, w_ref, wg_ref, bg_ref, q_ref, k_ref, v_ref, o_ref, gate_ref):
    xn = _rms(x_ref[...], g_ref[...]).astype(BF16)
    qk, dv = q_ref.shape[1], v_ref.shape[1]
    q_ref[...] = _dot(xn, _w(w_ref, cols=slice(0, qk))).astype(BF16)
    k_ref[...] = (_dot(xn, _w(w_ref, cols=slice(qk, 2 * qk))) * (ML_QK_DIM ** -0.5)).astype(BF16)
    v_ref[...] = _dot(xn, _w(w_ref, cols=slice(2 * qk, 2 * qk + dv))).astype(BF16)
    o_ref[...] = _dot(xn, _w(w_ref, cols=slice(2 * qk + dv, 2 * qk + 2 * dv))).astype(BF16)
    gates = _dot(xn, wg_ref[...]) + bg_ref[...]
    gates = ML_GATE_CAP * jnp.tanh(gates / ML_GATE_CAP)
    lane = lax.broadcasted_iota(jnp.int32, gates.shape, 1)
    gate_ref[...] = jnp.where(lane < ML_HEADS, gates, _log_sigmoid(gates))


def _ml_inproj(x, g, w_in, layer, wg, bg):
    m, d = x.shape
    tm = ROW_TILE
    qk = ML_HEADS * ML_QK_DIM
    row = lambda n: pl.BlockSpec((tm, n), lambda i: (i, 0))
    return pl.pallas_call(
        _ml_inproj_kernel,
        grid=(m // tm,),
        in_specs=[row(d), _const_spec((1, d)), _layer_spec(w_in, layer), _const_spec(wg.shape),
                  _const_spec((1, LANES))],
        out_specs=[row(qk), row(qk), row(d), row(d), row(LANES)],
        out_shape=[jax.ShapeDtypeStruct((m, qk), BF16),
                   jax.ShapeDtypeStruct((m, qk), BF16),
                   jax.ShapeDtypeStruct((m, d), BF16),
                   jax.ShapeDtypeStruct((m, d), BF16),
                   jax.ShapeDtypeStruct((m, LANES), F32)],
        compiler_params=_params("parallel"),
        name="ml_inproj",
    )(x, g, w_in, wg, bg)


def _mlstm_kernel(q_ref, k_ref, v_ref, o_ref, gc_ref, gr_ref, hg_ref, out_ref, ct_ref, n_ref, m_ref):
    L = ML_CHUNK

    @pl.when(pl.program_id(1) == 0)
    def _():
        ct_ref[...] = jnp.zeros_like(ct_ref)
        n_ref[...] = jnp.zeros_like(n_ref)
        m_ref[...] = jnp.zeros_like(m_ref)

    gc = gc_ref[0]
    gr = gr_ref[0]
    lane = lax.broadcasted_iota(jnp.int32, gc.shape, 1)
    t_idx = lax.broadcasted_iota(jnp.int32, (L, L), 0)
    s_idx = lax.broadcasted_iota(jnp.int32, (L, L), 1)
    causal = s_idx <= t_idx
    anti = t_idx <= s_idx

    for h in range(ML_HEADS):
        qk = slice(h * ML_QK_DIM, (h + 1) * ML_QK_DIM)
        vd = slice(h * ML_V_DIM, (h + 1) * ML_V_DIM)
        q = q_ref[0, :, qk]
        k = k_ref[0, :, qk]
        v = v_ref[0, :, vd]
        ig_col = jnp.sum(jnp.where(lane == h, gc, 0.0), axis=1, keepdims=True)
        lf_col = jnp.sum(jnp.where(lane == h + ML_HEADS, gc, 0.0), axis=1, keepdims=True)
        ig_row = gr[h:h + 1, :]
        lf_row = gr[h + ML_HEADS:h + ML_HEADS + 1, :]

        b_col = jnp.sum(jnp.where(causal, lf_row, 0.0), axis=1, keepdims=True)
        b_row = jnp.sum(jnp.where(anti, lf_col, 0.0), axis=0, keepdims=True)
        a_row = ig_row - b_row
        a_col = ig_col - b_col

        m_prev = m_ref[h]
        mx_col = jnp.maximum(jnp.max(jnp.where(causal, a_row, NEG_BIG), axis=1, keepdims=True), m_prev)
        mx_last = mx_col[L - 1:L, :]
        b_last = b_col[L - 1:L, :]

        dmat = jnp.where(causal, jnp.exp(jnp.minimum(a_row - mx_col, 0.0)), 0.0)
        sd = _dot_nt(q, k) * dmat
        inter = jnp.exp(m_prev - mx_col)
        ct = ct_ref[h]
        num = _dot(sd.astype(BF16), v) + inter * _dot(q, ct.astype(BF16))
        nq = jnp.sum(q.astype(F32) * n_ref[h], axis=1, keepdims=True)
        den = jnp.sum(sd, axis=1, keepdims=True) + inter * nq
        hh = num / jnp.maximum(jnp.abs(den), jnp.exp(-(b_col + mx_col)))

        hn = _rms(hh, hg_ref[h])
        out_ref[0, :, vd] = (hn * _sigmoid(o_ref[0, :, vd].astype(F32))).astype(out_ref.dtype)

        w_col = jnp.exp(a_col - mx_last)
        decay = jnp.exp(m_prev - mx_last)
        kw = k.astype(F32) * w_col
        ct_ref[h] = decay * ct + _dot(kw.T.astype(BF16), v)
        n_ref[h] = decay * n_ref[h] + jnp.sum(kw, axis=0, keepdims=True)
        m_ref[h] = b_last + mx_last


def _mlstm_core(q, k, v, o, gates_col, gates_row, head_gain):
    b, s, d = v.shape
    L = ML_CHUNK
    qk = ML_HEADS * ML_QK_DIM
    return pl.pallas_call(
        _mlstm_kernel,
        grid=(b, s // L),
        in_specs=[
            pl.BlockSpec((1, L, qk), lambda bi, ci: (bi, ci, 0)),
            pl.BlockSpec((1, L, qk), lambda bi, ci: (bi, ci, 0)),
            pl.BlockSpec((1, L, d), lambda bi, ci: (bi, ci, 0)),
            pl.BlockSpec((1, L, d), lambda bi, ci: (bi, ci, 0)),
            pl.BlockSpec((1, L, LANES), lambda bi, ci: (bi, ci, 0)),
            pl.BlockSpec((1, 2 * ML_HEADS, L), lambda bi, ci: (bi, 0, ci)),
            _const_spec((ML_HEADS, 1, ML_V_DIM)),
        ],
        out_specs=pl.BlockSpec((1, L, d), lambda bi, ci: (bi, ci, 0)),
        out_shape=jax.ShapeDtypeStruct(v.shape, BF16),
        scratch_shapes=[pltpu.VMEM((ML_HEADS, ML_QK_DIM, ML_V_DIM), F32),
                        pltpu.VMEM((ML_HEADS, 1, ML_QK_DIM), F32),
                        pltpu.VMEM((ML_HEADS, 1, 1), F32)],
        compiler_params=_params("parallel", "arbitrary"),
        name="mlstm_core",
    )(q, k, v, o, gates_col, gates_row, head_gain)


def _sb_inproj_kernel(x_ref, g_ref, w_ref, q_ref, k_ref, v_ref):
    xn = _rms(x_ref[...], g_ref[...]).astype(BF16)
    d = q_ref.shape[1]
    q_ref[...] = (_dot(xn, _w(w_ref, cols=slice(0, d))) * (LOG2_E * SB_HEAD_DIM ** -0.5)).astype(BF16)
    k_ref[...] = _dot(xn, _w(w_ref, cols=slice(d, 2 * d))).astype(BF16)
    v_ref[...] = _dot(xn, _w(w_ref, cols=slice(2 * d, 3 * d))).astype(BF16)


def _sb_inproj(x, g, w_qkv, layer):
    m, d = x.shape
    tm = ROW_TILE
    row = lambda n: pl.BlockSpec((tm, n), lambda i: (i, 0))
    return pl.pallas_call(
        _sb_inproj_kernel,
        grid=(m // tm,),
        in_specs=[row(d), _const_spec((1, d)), _layer_spec(w_qkv, layer)],
        out_specs=[row(d), row(d), row(d)],
        out_shape=[jax.ShapeDtypeStruct((m, d), BF16)] * 3,
        compiler_params=_params("parallel"),
        name="sb_inproj",
    )(x, g, w_qkv)


def _sb_kernel(q_ref, k_ref, v_ref, o_ref, z_ref, acc_ref, c_ref):
    qi = pl.program_id(2)
    T = SB_TILE
    lane = lax.broadcasted_iota(jnp.int32, (T, LANES), 1)
    first = lane < SB_HEAD_DIM
    row = lax.broadcasted_iota(jnp.int32, (T, T), 0)
    col = lax.broadcasted_iota(jnp.int32, (T, T), 1)
    neg_suffix = jnp.where(row >= col, -1.0, 0.0).astype(BF16)
    strict1 = col < row
    strict = jnp.concatenate([strict1, strict1], axis=0)

    cols = [slice(ch * LANES, (ch + 1) * LANES) for ch in range(SB_CHAINS)]
    qq = []
    for ch in range(SB_CHAINS):
        q = q_ref[0, :, cols[ch]]
        zero = jnp.zeros_like(q)
        qq.append(jnp.concatenate([jnp.where(first, q, zero), jnp.where(first, zero, q)], axis=0))

    def logits(idx, slot, diagonal=False):
        for ch in range(SB_CHAINS):
            kb = k_ref[0, pl.ds(pl.multiple_of(idx * T, T), T), cols[ch]]
            z = _rows2(lambda x: _dot_nt(x, kb), qq[ch])
            if diagonal:
                z = jnp.where(strict, z, NEG_BIG)
            z_ref[ch, slot] = z

    def weights(idx, slot):
        for ch in range(SB_CHAINS):
            vb = v_ref[0, pl.ds(pl.multiple_of(idx * T, T), T), cols[ch]]
            z = z_ref[ch, slot]
            sp = jnp.maximum(z, 0.0) + jnp.log2(1.0 + jnp.exp2(-jnp.abs(z)))
            cum = _rows2(lambda x: _dot(x, neg_suffix), sp.astype(BF16))
            carry = c_ref[ch]
            t = z + cum
            a = jnp.exp2(jnp.concatenate([t[:, :LANES] + carry, t[:, LANES:] + carry], axis=1))
            acc_ref[ch] += _rows2(lambda x: _dot(x, vb), a.astype(BF16))
            c_ref[ch] = carry + jnp.broadcast_to(cum[:, 0:1], carry.shape)

    acc_ref[...] = jnp.zeros_like(acc_ref)
    c_ref[...] = jnp.zeros_like(c_ref)
    odd = qi % 2

    @pl.when(odd == 0)
    def _():
        logits(qi, 0, diagonal=True)

    @pl.when(odd == 1)
    def _():
        logits(qi, 1, diagonal=True)
        logits(qi - 1, 0)
        weights(qi, 1)

    n_even = qi - odd

    def body(i, _):
        t = n_even - 2 * i
        logits(t - 1, 1)
        weights(t, 0)
        logits(t - 2, 0)
        weights(t - 1, 1)
        return 0

    lax.fori_loop(0, n_even // 2, body, 0)
    weights(0, 0)
    for ch in range(SB_CHAINS):
        acc = acc_ref[ch]
        o_ref[0, :, cols[ch]] = jnp.where(first, acc[:T], acc[T:]).astype(o_ref.dtype)


def _sb_core(q, k, v):
    b, s, d = q.shape
    T = SB_TILE
    w = SB_CHAINS * LANES
    return pl.pallas_call(
        _sb_kernel,
        grid=(b, d // w, s // T),
        in_specs=[pl.BlockSpec((1, T, w), lambda bi, hi, ti: (bi, ti, hi)),
                  pl.BlockSpec((1, s, w), lambda bi, hi, ti: (bi, 0, hi)),
                  pl.BlockSpec((1, s, w), lambda bi, hi, ti: (bi, 0, hi))],
        out_specs=pl.BlockSpec((1, T, w), lambda bi, hi, ti: (bi, ti, hi)),
        out_shape=jax.ShapeDtypeStruct(q.shape, BF16),
        scratch_shapes=[pltpu.VMEM((SB_CHAINS, 2, 2 * T, T), F32),
                        pltpu.VMEM((SB_CHAINS, 2 * T, LANES), F32),
                        pltpu.VMEM((SB_CHAINS, 2 * T, LANES), F32)],
        compiler_params=_params("parallel", "parallel", "arbitrary"),
        name="sb_core",
    )(q, k, v)


def _outproj_kernel(a_ref, w_ref, g_ref, x_ref, o_ref):
    y = _dot(a_ref[...], _w(w_ref))
    o_ref[...] = x_ref[...] + _rms(y, g_ref[...])


def _outproj(a, w, layer, g, x):
    m, d = x.shape
    kdim = a.shape[1]
    tm = ROW_TILE
    return pl.pallas_call(
        _outproj_kernel,
        grid=(m // tm,),
        in_specs=[pl.BlockSpec((tm, kdim), lambda i: (i, 0)), _layer_spec(w, layer), _const_spec((1, d)),
                  pl.BlockSpec((tm, d), lambda i: (i, 0))],
        out_specs=pl.BlockSpec((tm, d), lambda i: (i, 0)),
        out_shape=jax.ShapeDtypeStruct((m, d), F32),
        compiler_params=_params("parallel"),
        name="outproj",
    )(a, w, g, x)


def _memkv_kernel(mem_ref, g_ref, w_ref, o_ref):
    mn = _rms(mem_ref[...], g_ref[...]).astype(BF16)
    o_ref[0] = _dot(mn, _w(w_ref)).astype(o_ref.dtype)


def _memkv(mem, g, w_kv):
    m, d = mem.shape
    depth, _, n = w_kv.shape
    tm = ROW_TILE
    return pl.pallas_call(
        _memkv_kernel,
        grid=(depth, m // tm),
        in_specs=[pl.BlockSpec((tm, d), lambda l, i: (i, 0)), _const_spec((1, d)),
                  pl.BlockSpec((1, d, n), lambda l, i: (l, 0, 0))],
        out_specs=pl.BlockSpec((1, tm, n), lambda l, i: (l, i, 0)),
        out_shape=jax.ShapeDtypeStruct((depth, m, n), BF16),
        compiler_params=_params("parallel", "parallel"),
        name="memkv",
    )(mem, g, w_kv)


def _xattn_kernel(x_ref, gpre_ref, gpost_ref, wq_ref, kv_ref, wo_ref, o_ref):
    x = x_ref[0]
    xn = _rms(x, gpre_ref[...]).astype(BF16)
    q = (_dot(xn, _w(wq_ref)) * (XA_HEAD_DIM ** -0.5)).astype(BF16)
    d = XA_HEADS * XA_HEAD_DIM
    heads = []
    for h in range(XA_HEADS):
        lo, hi = h * XA_HEAD_DIM, (h + 1) * XA_HEAD_DIM
        s = _dot_nt(q[:, lo:hi], kv_ref[0, 0, :, lo:hi])
        p = jnp.exp(s - jnp.max(s, axis=-1, keepdims=True))
        denom = jnp.sum(p, axis=-1, keepdims=True)
        heads.append((_dot(p.astype(BF16), kv_ref[0, 0, :, d + lo:d + hi]) / denom).astype(BF16))
    attn = jnp.concatenate(heads, axis=-1)
    y = _dot(attn, _w(wo_ref))
    o_ref[0] = x + _rms(y, gpost_ref[...])


def _xattn(x, gpre, gpost, wq, kv, layer, wo):
    b, s, d = x.shape
    n_mem = kv.shape[2]
    tm = ROW_TILE
    return pl.pallas_call(
        _xattn_kernel,
        grid=(b, s // tm),
        in_specs=[pl.BlockSpec((1, tm, d), lambda bi, i: (bi, i, 0)), _const_spec((1, d)),
                  _const_spec((1, d)), _layer_spec(wq, layer),
                  pl.BlockSpec((1, 1, n_mem, 2 * d), lambda bi, i: (layer, bi, 0, 0)),
                  _layer_spec(wo, layer)],
        out_specs=pl.BlockSpec((1, tm, d), lambda bi, i: (bi, i, 0)),
        out_shape=jax.ShapeDtypeStruct(x.shape, F32),
        compiler_params=_params("parallel", "parallel"),
        name="xattn",
    )(x, gpre, gpost, wq, kv, wo)


def _ffn_kernel(x_ref, gpre_ref, gpost_ref, wgu_ref, wd_ref, o_ref):
    x = x_ref[...]
    xn = _rms(x, gpre_ref[...]).astype(BF16)
    d_ff = wd_ref.shape[1]
    y = jnp.zeros(x.shape, F32)
    for c0 in range(0, d_ff, FF_CHUNK):
        gate = _dot(xn, _w(wgu_ref, cols=slice(c0, c0 + FF_CHUNK)))
        up = _dot(xn, _w(wgu_ref, cols=slice(d_ff + c0, d_ff + c0 + FF_CHUNK)))
        hid = (gate * _sigmoid(gate) * up).astype(BF16)
        y = y + _dot(hid, _w(wd_ref, rows=slice(c0, c0 + FF_CHUNK)))
    o_ref[...] = x + _rms(y, gpost_ref[...])


def _ffn(x, gpre, gpost, w_gate_up, w_down, layer):
    m, d = x.shape
    tm = FFN_ROW_TILE
    return pl.pallas_call(
        _ffn_kernel,
        grid=(m // tm,),
        in_specs=[pl.BlockSpec((tm, d), lambda i: (i, 0)), _const_spec((1, d)), _const_spec((1, d)),
                  _layer_spec(w_gate_up, layer), _layer_spec(w_down, layer)],
        out_specs=pl.BlockSpec((tm, d), lambda i: (i, 0)),
        out_shape=jax.ShapeDtypeStruct((m, d), F32),
        compiler_params=_params("parallel"),
        name="ffn",
    )(x, gpre, gpost, w_gate_up, w_down)


def kernel(x, mem, mem_norm_gain, norm_gains, ml_w_in, ml_b_gate, ml_head_gain, ml_w_out, sb_w_qkv, sb_w_out,
           xa_w_q, xa_w_kv, xa_w_o, ffn_w_gate_up, ffn_w_down):
    b, s, d = x.shape
    depth = norm_gains.shape[0]
    n_mem = mem.shape[1]
    m = b * s
    qk = ML_HEADS * ML_QK_DIM
    n_gates = 2 * ML_HEADS
    gains = norm_gains.reshape(depth, norm_gains.shape[1], 1, d)

    kv = _memkv(mem.reshape(b * n_mem, d), mem_norm_gain.reshape(1, d), xa_w_kv)
    kv = kv.reshape(depth, b, n_mem, 2 * d)

    xf = x.reshape(m, d)
    for layer in range(depth):
        g = gains[layer]
        j = layer // 2
        if layer % 2 == 0:
            wg = jnp.pad(ml_w_in[j, :, 2 * qk + 2 * d:], ((0, 0), (0, LANES - n_gates))).astype(BF16)
            bg = jnp.pad(ml_b_gate[j], (0, LANES - n_gates)).reshape(1, LANES)
            q, k, v, o, gates = _ml_inproj(xf, g[0], ml_w_in, j, wg, bg)
            gates = gates.reshape(b, s, LANES)
            gates_row = jnp.swapaxes(gates[:, :, :n_gates], 1, 2)
            mixed = _mlstm_core(q.reshape(b, s, qk), k.reshape(b, s, qk), v.reshape(b, s, d),
                                o.reshape(b, s, d), gates, gates_row,
                                ml_head_gain[j].reshape(ML_HEADS, 1, ML_V_DIM))
            xf = _outproj(mixed.reshape(m, d), ml_w_out, j, g[1], xf)
        else:
            q, k, v = _sb_inproj(xf, g[0], sb_w_qkv, j)
            mixed = _sb_core(q.reshape(b, s, d), k.reshape(b, s, d), v.reshape(b, s, d))
            xf = _outproj(mixed.reshape(m, d), sb_w_out, j, g[1], xf)
        xf = _xattn(xf.reshape(b, s, d), g[2], g[3], xa_w_q, kv, layer, xa_w_o).reshape(m, d)
        xf = _ffn(xf, g[4], g[5], ffn_w_gate_up, ffn_w_down, layer)
    return xf.reshape(b, s, d)
```

```python
import functools

import jax
import jax.numpy as jnp
from jax import lax
from jax.experimental import pallas as pl
from jax.experimental.pallas import tpu as pltpu

F32 = jnp.float32
BF16 = jnp.bfloat16

RMS_EPS = 1e-6
ML_HEADS = 4
ML_QK_DIM = 128
ML_V_DIM = 256
ML_GATE_CAP = 15.0
SB_HEADS = 16
SB_HEAD_DIM = 64
XA_HEADS = 4
XA_HEAD_DIM = 256

LANES = 128
ROW_TILE = 1024
FFN_ROW_TILE = 512
ML_CHUNK = 256
ML_BATCH = 1
SB_TILE = 256
SB_CHAINS = 8
FF_CHUNK = 256
VMEM_LIMIT = 56 * 1024 * 1024
NEG_BIG = -1e30
LOG2_E = 1.4426950408889634


def _rms(x, g):
    ms = jnp.mean(x * x, axis=-1, keepdims=True)
    return x * lax.rsqrt(ms + RMS_EPS) * g


def _sigmoid(x):
    return 1.0 / (1.0 + jnp.exp(-x))


def _log_sigmoid(x):
    return jnp.minimum(x, 0.0) - jnp.log1p(jnp.exp(-jnp.abs(x)))


def _dot(a, b):
    return jnp.dot(a, b, preferred_element_type=F32)


def _dot_nt(a, b):
    return lax.dot_general(a, b, (((1,), (1,)), ((), ())), preferred_element_type=F32)


def _rows2(fn, x):
    half = x.shape[0] // 2
    return jnp.concatenate([fn(x[:half]), fn(x[half:])], axis=0)


def _params(*sem):
    return pltpu.CompilerParams(dimension_semantics=sem, vmem_limit_bytes=VMEM_LIMIT)


def _const_spec(shape):
    return pl.BlockSpec(shape, lambda *_: (0,) * len(shape), pipeline_mode=pl.Buffered(1))


def _layer_spec(w, layer):
    return pl.BlockSpec((1,) + w.shape[1:], lambda *_: (layer, 0, 0), pipeline_mode=pl.Buffered(1))


def _w(ref, rows=slice(None), cols=slice(None)):
    return ref[0, rows, cols].astype(BF16)


def _ml_inproj_kernel(x_ref, g_ref, w_ref, wg_ref, bg_ref, q_ref, k_ref, v_ref, o_ref, gate_ref):
    xn = _rms(x_ref[...], g_ref[...]).astype(BF16)
    qk, dv = q_ref.shape[1], v_ref.shape[1]
    q_ref[...] = _dot(xn, _w(w_ref, cols=slice(0, qk))).astype(BF16)
    k_ref[...] = (_dot(xn, _w(w_ref, cols=slice(qk, 2 * qk))) * (ML_QK_DIM ** -0.5)).astype(BF16)
    v_ref[...] = _dot(xn, _w(w_ref, cols=slice(2 * qk, 2 * qk + dv))).astype(BF16)
    o_ref[...] = _dot(xn, _w(w_ref, cols=slice(2 * qk + dv, 2 * qk + 2 * dv))).astype(BF16)
    gates = _dot(xn, wg_ref[...]) + bg_ref[...]
    gates = ML_GATE_CAP * jnp.tanh(gates / ML_GATE_CAP)
    lane = lax.broadcasted_iota(jnp.int32, gates.shape, 1)
    gate_ref[...] = jnp.where(lane < ML_HEADS, gates, _log_sigmoid(gates))


def _ml_inproj(x, g, w_in, layer, wg, bg):
    m, d = x.shape
    tm = ROW_TILE
    qk = ML_HEADS * ML_QK_DIM
    row = lambda n: pl.BlockSpec((tm, n), lambda i: (i, 0))
    return pl.pallas_call(
        _ml_inproj_kernel,
        grid=(m // tm,),
        in_specs=[row(d), _const_spec((1, d)), _layer_spec(w_in, layer), _const_spec(wg.shape),
                  _const_spec((1, LANES))],
        out_specs=[row(qk), row(qk), row(d), row(d), row(LANES)],
        out_shape=[jax.ShapeDtypeStruct((m, qk), BF16),
                   jax.ShapeDtypeStruct((m, qk), BF16),
                   jax.ShapeDtypeStruct((m, d), BF16),
                   jax.ShapeDtypeStruct((m, d), BF16),
                   jax.ShapeDtypeStruct((m, LANES), F32)],
        compiler_params=_params("parallel"),
        name="ml_inproj",
    )(x, g, w_in, wg, bg)


def _mlstm_kernel(q_ref, k_ref, v_ref, o_ref, gc_ref, gr_ref, hg_ref, out_ref, ct_ref, n_ref, m_ref):
    L = ML_CHUNK

    @pl.when(pl.program_id(1) == 0)
    def _():
        ct_ref[...] = jnp.zeros_like(ct_ref)
        n_ref[...] = jnp.zeros_like(n_ref)
        m_ref[...] = jnp.zeros_like(m_ref)

    lane = lax.broadcasted_iota(jnp.int32, (L, LANES), 1)
    t_idx = lax.broadcasted_iota(jnp.int32, (L, L), 0)
    s_idx = lax.broadcasted_iota(jnp.int32, (L, L), 1)
    causal = s_idx <= t_idx
    anti = t_idx <= s_idx

    for r in range(ML_BATCH * ML_HEADS):
        bb, h = divmod(r, ML_HEADS)
        qk = slice(h * ML_QK_DIM, (h + 1) * ML_QK_DIM)
        vd = slice(h * ML_V_DIM, (h + 1) * ML_V_DIM)
        q = q_ref[bb, :, qk]
        k = k_ref[bb, :, qk]
        v = v_ref[bb, :, vd]
        gc = gc_ref[bb]
        gr = gr_ref[bb]
        ig_col = jnp.sum(jnp.where(lane == h, gc, 0.0), axis=1, keepdims=True)
        lf_col = jnp.sum(jnp.where(lane == h + ML_HEADS, gc, 0.0), axis=1, keepdims=True)
        ig_row = gr[h:h + 1, :]
        lf_row = gr[h + ML_HEADS:h + ML_HEADS + 1, :]

        b_col = jnp.sum(jnp.where(causal, lf_row, 0.0), axis=1, keepdims=True)
        b_row = jnp.sum(jnp.where(anti, lf_col, 0.0), axis=0, keepdims=True)
        a_row = ig_row - b_row
        a_col = ig_col - b_col

        m_prev = m_ref[r]
        mx_col = jnp.maximum(jnp.max(jnp.where(causal, a_row, NEG_BIG), axis=1, keepdims=True), m_prev)
        mx_last = mx_col[L - 1:L, :]
        b_last = b_col[L - 1:L, :]

        dmat = jnp.where(causal, jnp.exp(jnp.minimum(a_row - mx_col, 0.0)), 0.0)
        sd = _dot_nt(q, k) * dmat
        inter = jnp.exp(m_prev - mx_col)
        ct = ct_ref[r]
        num = _dot(sd.astype(BF16), v) + inter * _dot(q, ct.astype(BF16))
        nq = jnp.sum(q.astype(F32) * n_ref[r], axis=1, keepdims=True)
        den = jnp.sum(sd, axis=1, keepdims=True) + inter * nq
        hh = num / jnp.maximum(jnp.abs(den), jnp.exp(-(b_col + mx_col)))

        hn = _rms(hh, hg_ref[h])
        out_ref[bb, :, vd] = (hn * _sigmoid(o_ref[bb, :, vd].astype(F32))).astype(out_ref.dtype)

        w_col = jnp.exp(a_col - mx_last)
        decay = jnp.exp(m_prev - mx_last)
        kw = k.astype(F32) * w_col
        ct_ref[r] = decay * ct + _dot(kw.T.astype(BF16), v)
        n_ref[r] = decay * n_ref[r] + jnp.sum(kw, axis=0, keepdims=True)
        m_ref[r] = b_last + mx_last


def _mlstm_core(q, k, v, o, gates_col, gates_row, head_gain):
    b, s, d = v.shape
    L = ML_CHUNK
    qk = ML_HEADS * ML_QK_DIM
    nb = ML_BATCH
    n_rec = nb * ML_HEADS
    return pl.pallas_call(
        _mlstm_kernel,
        grid=(b // nb, s // L),
        in_specs=[
            pl.BlockSpec((nb, L, qk), lambda bi, ci: (bi, ci, 0)),
            pl.BlockSpec((nb, L, qk), lambda bi, ci: (bi, ci, 0)),
            pl.BlockSpec((nb, L, d), lambda bi, ci: (bi, ci, 0)),
            pl.BlockSpec((nb, L, d), lambda bi, ci: (bi, ci, 0)),
            pl.BlockSpec((nb, L, LANES), lambda bi, ci: (bi, ci, 0)),
            pl.BlockSpec((nb, 2 * ML_HEADS, L), lambda bi, ci: (bi, 0, ci)),
            _const_spec((ML_HEADS, 1, ML_V_DIM)),
        ],
        out_specs=pl.BlockSpec((nb, L, d), lambda bi, ci: (bi, ci, 0)),
        out_shape=jax.ShapeDtypeStruct(v.shape, BF16),
        scratch_shapes=[pltpu.VMEM((n_rec, ML_QK_DIM, ML_V_DIM), F32),
                        pltpu.VMEM((n_rec, 1, ML_QK_DIM), F32),
                        pltpu.VMEM((n_rec, 1, 1), F32)],
        compiler_params=_params("parallel", "arbitrary"),
        name="mlstm_core",
    )(q, k, v, o, gates_col, gates_row, head_gain)


def _sb_inproj_kernel(x_ref, g_ref, w_ref, q_ref, k_ref, v_ref):
    xn = _rms(x_ref[...], g_ref[...]).astype(BF16)
    d = q_ref.shape[1]
    q_ref[...] = (_dot(xn, _w(w_ref, cols=slice(0, d))) * (LOG2_E * SB_HEAD_DIM ** -0.5)).astype(BF16)
    k_ref[...] = _dot(xn, _w(w_ref, cols=slice(d, 2 * d))).astype(BF16)
    v_ref[...] = _dot(xn, _w(w_ref, cols=slice(2 * d, 3 * d))).astype(BF16)


def _sb_inproj(x, g, w_qkv, layer):
    m, d = x.shape
    tm = ROW_TILE
    row = lambda n: pl.BlockSpec((tm, n), lambda i: (i, 0))
    return pl.pallas_call(
        _sb_inproj_kernel,
        grid=(m // tm,),
        in_specs=[row(d), _const_spec((1, d)), _layer_spec(w_qkv, layer)],
        out_specs=[row(d), row(d), row(d)],
        out_shape=[jax.ShapeDtypeStruct((m, d), BF16)] * 3,
        compiler_params=_params("parallel"),
        name="sb_inproj",
    )(x, g, w_qkv)


def _sb_kernel(q_ref, k_ref, v_ref, o_ref, z_ref, acc_ref, c_ref):
    qi = pl.program_id(2)
    T = SB_TILE
    lane = lax.broadcasted_iota(jnp.int32, (T, LANES), 1)
    first = lane < SB_HEAD_DIM
    row = lax.broadcasted_iota(jnp.int32, (T, T), 0)
    col = lax.broadcasted_iota(jnp.int32, (T, T), 1)
    neg_suffix = jnp.where(row >= col, -1.0, 0.0).astype(BF16)
    strict1 = col < row
    strict = jnp.concatenate([strict1, strict1], axis=0)

    cols = [slice(ch * LANES, (ch + 1) * LANES) for ch in range(SB_CHAINS)]
    qq = []
    for ch in range(SB_CHAINS):
        q = q_ref[0, :, cols[ch]]
        zero = jnp.zeros_like(q)
        qq.append(jnp.concatenate([jnp.where(first, q, zero), jnp.where(first, zero, q)], axis=0))

    def logits(idx, slot, diagonal=False):
        for ch in range(SB_CHAINS):
            kb = k_ref[0, pl.ds(pl.multiple_of(idx * T, T), T), cols[ch]]
            z = _rows2(lambda x: _dot_nt(x, kb), qq[ch])
            if diagonal:
                z = jnp.where(strict, z, NEG_BIG)
            z_ref[ch, slot] = z

    def weights(idx, slot):
        for ch in range(SB_CHAINS):
            vb = v_ref[0, pl.ds(pl.multiple_of(idx * T, T), T), cols[ch]]
            z = z_ref[ch, slot]
            sp = jnp.maximum(z, 0.0) + jnp.log2(1.0 + jnp.exp2(-jnp.abs(z)))
            cum = _rows2(lambda x: _dot(x, neg_suffix), sp.astype(BF16))
            carry = c_ref[ch]
            t = z + cum
            a = jnp.exp2(jnp.concatenate([t[:, :LANES] + carry, t[:, LANES:] + carry], axis=1))
            acc_ref[ch] += _rows2(lambda x: _dot(x, vb), a.astype(BF16))
            c_ref[ch] = carry + jnp.broadcast_to(cum[:, 0:1], carry.shape)

    acc_ref[...] = jnp.zeros_like(acc_ref)
    c_ref[...] = jnp.zeros_like(c_ref)
    odd = qi % 2

    @pl.when(odd == 0)
    def _():
        logits(qi, 0, diagonal=True)

    @pl.when(odd == 1)
    def _():
        logits(qi, 1, diagonal=True)
        logits(qi - 1, 0)
        weights(qi, 1)

    n_even = qi - odd

    def body(i, _):
        t = n_even - 2 * i
        logits(t - 1, 1)
        weights(t, 0)
        logits(t - 2, 0)
        weights(t - 1, 1)
        return 0

    lax.fori_loop(0, n_even // 2, body, 0)
    weights(0, 0)
    for ch in range(SB_CHAINS):
        acc = acc_ref[ch]
        o_ref[0, :, cols[ch]] = jnp.where(first, acc[:T], acc[T:]).astype(o_ref.dtype)


def _sb_core(q, k, v):
    b, s, d = q.shape
    T = SB_TILE
    w = SB_CHAINS * LANES
    return pl.pallas_call(
        _sb_kernel,
        grid=(b, d // w, s // T),
        in_specs=[pl.BlockSpec((1, T, w), lambda bi, hi, ti: (bi, ti, hi)),
                  pl.BlockSpec((1, s, w), lambda bi, hi, ti: (bi, 0, hi)),
                  pl.BlockSpec((1, s, w), lambda bi, hi, ti: (bi, 0, hi))],
        out_specs=pl.BlockSpec((1, T, w), lambda bi, hi, ti: (bi, ti, hi)),
        out_shape=jax.ShapeDtypeStruct(q.shape, BF16),
        scratch_shapes=[pltpu.VMEM((SB_CHAINS, 2, 2 * T, T), F32),
                        pltpu.VMEM((SB_CHAINS, 2 * T, LANES), F32),
                        pltpu.VMEM((SB_CHAINS, 2 * T, LANES), F32)],
        compiler_params=_params("parallel", "parallel", "arbitrary"),
        name="sb_core",
    )(q, k, v)


def _memkv_kernel(mem_ref, g_ref, w_ref, o_ref):
    mn = _rms(mem_ref[...], g_ref[...]).astype(BF16)
    o_ref[0] = _dot(mn, _w(w_ref)).astype(o_ref.dtype)


def _memkv(mem, g, w_kv):
    m, d = mem.shape
    depth, _, n = w_kv.shape
    tm = ROW_TILE
    return pl.pallas_call(
        _memkv_kernel,
        grid=(depth, m // tm),
        in_specs=[pl.BlockSpec((tm, d), lambda l, i: (i, 0)), _const_spec((1, d)),
                  pl.BlockSpec((1, d, n), lambda l, i: (l, 0, 0))],
        out_specs=pl.BlockSpec((1, tm, n), lambda l, i: (l, i, 0)),
        out_shape=jax.ShapeDtypeStruct((depth, m, n), BF16),
        compiler_params=_params("parallel", "parallel"),
        name="memkv",
    )(mem, g, w_kv)


def _xattn_kernel(a_ref, wmix_ref, gmix_ref, x_ref, gpre_ref, gpost_ref, wq_ref, kv_ref, wo_ref, o_ref):
    x = x_ref[0] + _rms(_dot(a_ref[0], _w(wmix_ref)), gmix_ref[...])
    xn = _rms(x, gpre_ref[...]).astype(BF16)
    q = (_dot(xn, _w(wq_ref)) * (XA_HEAD_DIM ** -0.5)).astype(BF16)
    d = XA_HEADS * XA_HEAD_DIM
    heads = []
    for h in range(XA_HEADS):
        lo, hi = h * XA_HEAD_DIM, (h + 1) * XA_HEAD_DIM
        s = _dot_nt(q[:, lo:hi], kv_ref[0, 0, :, lo:hi])
        p = jnp.exp(s - jnp.max(s, axis=-1, keepdims=True))
        denom = jnp.sum(p, axis=-1, keepdims=True)
        heads.append((_dot(p.astype(BF16), kv_ref[0, 0, :, d + lo:d + hi]) / denom).astype(BF16))
    attn = jnp.concatenate(heads, axis=-1)
    y = _dot(attn, _w(wo_ref))
    o_ref[0] = x + _rms(y, gpost_ref[...])


def _xattn(mixed, w_mix, mix_layer, gmix, x, gpre, gpost, wq, kv, layer, wo):
    b, s, d = x.shape
    n_mem = kv.shape[2]
    tm = ROW_TILE
    rows = pl.BlockSpec((1, tm, d), lambda bi, i: (bi, i, 0))
    return pl.pallas_call(
        _xattn_kernel,
        grid=(b, s // tm),
        in_specs=[rows, _layer_spec(w_mix, mix_layer), _const_spec((1, d)),
                  rows, _const_spec((1, d)), _const_spec((1, d)), _layer_spec(wq, layer),
                  pl.BlockSpec((1, 1, n_mem, 2 * d), lambda bi, i: (layer, bi, 0, 0)),
                  _layer_spec(wo, layer)],
        out_specs=rows,
        out_shape=jax.ShapeDtypeStruct(x.shape, F32),
        compiler_params=_params("parallel", "parallel"),
        name="xattn",
    )(mixed, w_mix, gmix, x, gpre, gpost, wq, kv, wo)


def _ffn_kernel(x_ref, gpre_ref, gpost_ref, wgu_ref, wd_ref, o_ref):
    x = x_ref[...]
    xn = _rms(x, gpre_ref[...]).astype(BF16)
    d_ff = wd_ref.shape[1]
    y = jnp.zeros(x.shape, F32)
    for c0 in range(0, d_ff, FF_CHUNK):
        gate = _dot(xn, _w(wgu_ref, cols=slice(c0, c0 + FF_CHUNK)))
        up = _dot(xn, _w(wgu_ref, cols=slice(d_ff + c0, d_ff + c0 + FF_CHUNK)))
        hid = (gate * _sigmoid(gate) * up).astype(BF16)
        y = y + _dot(hid, _w(wd_ref, rows=slice(c0, c0 + FF_CHUNK)))
    o_ref[...] = x + _rms(y, gpost_ref[...])


def _ffn(x, gpre, gpost, w_gate_up, w_down, layer):
    m, d = x.shape
    tm = FFN_ROW_TILE
    return pl.pallas_call(
        _ffn_kernel,
        grid=(m // tm,),
        in_specs=[pl.BlockSpec((tm, d), lambda i: (i, 0)), _const_spec((1, d)), _const_spec((1, d)),
                  _layer_spec(w_gate_up, layer), _layer_spec(w_down, layer)],
        out_specs=pl.BlockSpec((tm, d), lambda i: (i, 0)),
        out_shape=jax.ShapeDtypeStruct((m, d), F32),
        compiler_params=_params("parallel"),
        name="ffn",
    )(x, gpre, gpost, w_gate_up, w_down)


def kernel(x, mem, mem_norm_gain, norm_gains, ml_w_in, ml_b_gate, ml_head_gain, ml_w_out, sb_w_qkv, sb_w_out,
           xa_w_q, xa_w_kv, xa_w_o, ffn_w_gate_up, ffn_w_down):
    b, s, d = x.shape
    depth = norm_gains.shape[0]
    n_mem = mem.shape[1]
    m = b * s
    qk = ML_HEADS * ML_QK_DIM
    n_gates = 2 * ML_HEADS
    gains = norm_gains.reshape(depth, norm_gains.shape[1], 1, d)

    kv = _memkv(mem.reshape(b * n_mem, d), mem_norm_gain.reshape(1, d), xa_w_kv)
    kv = kv.reshape(depth, b, n_mem, 2 * d)

    xf = x.reshape(m, d)
    for layer in range(depth):
        g = gains[layer]
        j = layer // 2
        if layer % 2 == 0:
            wg = jnp.pad(ml_w_in[j, :, 2 * qk + 2 * d:], ((0, 0), (0, LANES - n_gates))).astype(BF16)
            bg = jnp.pad(ml_b_gate[j], (0, LANES - n_gates)).reshape(1, LANES)
            q, k, v, o, gates = _ml_inproj(xf, g[0], ml_w_in, j, wg, bg)
            gates = gates.reshape(b, s, LANES)
            gates_row = jnp.swapaxes(gates[:, :, :n_gates], 1, 2)
            mixed = _mlstm_core(q.reshape(b, s, qk), k.reshape(b, s, qk), v.reshape(b, s, d),
                                o.reshape(b, s, d), gates, gates_row,
                                ml_head_gain[j].reshape(ML_HEADS, 1, ML_V_DIM))
            w_mix = ml_w_out
        else:
            q, k, v = _sb_inproj(xf, g[0], sb_w_qkv, j)
            mixed = _sb_core(q.reshape(b, s, d), k.reshape(b, s, d), v.reshape(b, s, d))
            w_mix = sb_w_out
        xf = _xattn(mixed, w_mix, j, g[1], xf.reshape(b, s, d), g[2], g[3], xa_w_q, kv, layer,
                    xa_w_o).reshape(m, d)
        xf = _ffn(xf, g[4], g[5], ffn_w_gate_up, ffn_w_down, layer)
    return xf.reshape(b, s, d)
```

```python
import functools

import jax
import jax.numpy as jnp
from jax import lax
from jax.experimental import pallas as pl
from jax.experimental.pallas import tpu as pltpu

F32 = jnp.float32
BF16 = jnp.bfloat16

RMS_EPS = 1e-6
ML_HEADS = 4
ML_QK_DIM = 128
ML_V_DIM = 256
ML_GATE_CAP = 15.0
SB_HEADS = 16
SB_HEAD_DIM = 64
XA_HEADS = 4
XA_HEAD_DIM = 256

LANES = 128
ROW_TILE = 1024
FFN_ROW_TILE = 512
ML_CHUNK = 256
ML_BATCH = 1
SB_TILE = 256
SB_CHAINS = 8
FF_CHUNK = 256
VMEM_LIMIT = 56 * 1024 * 1024
NEG_BIG = -1e30
LOG2_E = 1.4426950408889634


def _rms(x, g):
    ms = jnp.mean(x * x, axis=-1, keepdims=True)
    return x * lax.rsqrt(ms + RMS_EPS) * g


def _sigmoid(x):
    return 1.0 / (1.0 + jnp.exp(-x))


def _log_sigmoid(x):
    return jnp.minimum(x, 0.0) - jnp.log1p(jnp.exp(-jnp.abs(x)))


def _dot(a, b):
    return jnp.dot(a, b, preferred_element_type=F32)


def _dot_nt(a, b):
    return lax.dot_general(a, b, (((1,), (1,)), ((), ())), preferred_element_type=F32)


def _rows2(fn, x):
    half = x.shape[0] // 2
    return jnp.concatenate([fn(x[:half]), fn(x[half:])], axis=0)


def _params(*sem):
    return pltpu.CompilerParams(dimension_semantics=sem, vmem_limit_bytes=VMEM_LIMIT)


def _const_spec(shape):
    return pl.BlockSpec(shape, lambda *_: (0,) * len(shape), pipeline_mode=pl.Buffered(1))


def _layer_spec(w, layer):
    return pl.BlockSpec((1,) + w.shape[1:], lambda *_: (layer, 0, 0), pipeline_mode=pl.Buffered(1))


def _w(ref, rows=slice(None), cols=slice(None)):
    return ref[0, rows, cols].astype(BF16)


def _ml_inproj_kernel(x_ref, g_ref, w_ref, wg_ref, bg_ref, q_ref, k_ref, v_ref, o_ref, gate_ref):
    xn = _rms(x_ref[...], g_ref[...]).astype(BF16)
    qk, dv = q_ref.shape[1], v_ref.shape[1]
    q_ref[...] = _dot(xn, _w(w_ref, cols=slice(0, qk))).astype(BF16)
    k_ref[...] = (_dot(xn, _w(w_ref, cols=slice(qk, 2 * qk))) * (ML_QK_DIM ** -0.5)).astype(BF16)
    v_ref[...] = _dot(xn, _w(w_ref, cols=slice(2 * qk, 2 * qk + dv))).astype(BF16)
    o_ref[...] = _dot(xn, _w(w_ref, cols=slice(2 * qk + dv, 2 * qk + 2 * dv))).astype(BF16)
    gates = _dot(xn, wg_ref[...]) + bg_ref[...]
    gates = ML_GATE_CAP * jnp.tanh(gates / ML_GATE_CAP)
    lane = lax.broadcasted_iota(jnp.int32, gates.shape, 1)
    gate_ref[...] = jnp.where(lane < ML_HEADS, gates, _log_sigmoid(gates))


def _ml_inproj(x, g, w_in, layer, wg, bg):
    m, d = x.shape
    tm = ROW_TILE
    qk = ML_HEADS * ML_QK_DIM
    row = lambda n: pl.BlockSpec((tm, n), lambda i: (i, 0))
    return pl.pallas_call(
        _ml_inproj_kernel,
        grid=(m // tm,),
        in_specs=[row(d), _const_spec((1, d)), _layer_spec(w_in, layer), _const_spec(wg.shape),
                  _const_spec((1, LANES))],
        out_specs=[row(qk), row(qk), row(d), row(d), row(LANES)],
        out_shape=[jax.ShapeDtypeStruct((m, qk), BF16),
                   jax.ShapeDtypeStruct((m, qk), BF16),
                   jax.ShapeDtypeStruct((m, d), BF16),
                   jax.ShapeDtypeStruct((m, d), BF16),
                   jax.ShapeDtypeStruct((m, LANES), F32)],
        compiler_params=_params("parallel"),
        name="ml_inproj",
    )(x, g, w_in, wg, bg)


def _mlstm_kernel(q_ref, k_ref, v_ref, o_ref, gc_ref, gr_ref, hg_ref, out_ref, ct_ref, n_ref, m_ref):
    L = ML_CHUNK

    @pl.when(pl.program_id(1) == 0)
    def _():
        ct_ref[...] = jnp.zeros_like(ct_ref)
        n_ref[...] = jnp.zeros_like(n_ref)
        m_ref[...] = jnp.zeros_like(m_ref)

    sel_row = lax.broadcasted_iota(jnp.int32, (LANES, LANES), 0)
    t_idx = lax.broadcasted_iota(jnp.int32, (L, L), 0)
    s_idx = lax.broadcasted_iota(jnp.int32, (L, L), 1)
    causal = s_idx <= t_idx
    anti = t_idx <= s_idx

    for r in range(ML_BATCH * ML_HEADS):
        bb, h = divmod(r, ML_HEADS)
        qk = slice(h * ML_QK_DIM, (h + 1) * ML_QK_DIM)
        vd = slice(h * ML_V_DIM, (h + 1) * ML_V_DIM)
        q = q_ref[bb, :, qk]
        k = k_ref[bb, :, qk]
        v = v_ref[bb, :, vd]
        gc = gc_ref[bb]
        gr = gr_ref[bb]
        g_hi = gc.astype(BF16)
        rest = gc - g_hi.astype(F32)
        g_mid = rest.astype(BF16)
        g_lo = (rest - g_mid.astype(F32)).astype(BF16)
        pick_i = jnp.where(sel_row == h, 1.0, 0.0).astype(BF16)
        pick_f = jnp.where(sel_row == h + ML_HEADS, 1.0, 0.0).astype(BF16)
        ig_col = _dot(g_hi, pick_i) + _dot(g_mid, pick_i) + _dot(g_lo, pick_i)
        lf_col = _dot(g_hi, pick_f) + _dot(g_mid, pick_f) + _dot(g_lo, pick_f)
        ig_row = gr[h:h + 1, :]
        lf_row = gr[h + ML_HEADS:h + ML_HEADS + 1, :]

        b_col = jnp.sum(jnp.where(causal, lf_row, 0.0), axis=1, keepdims=True)
        lf_wide = jnp.concatenate([lf_col] * (L // LANES), axis=1)
        b_row = jnp.sum(jnp.where(anti, lf_wide, 0.0), axis=0, keepdims=True)
        a_row = ig_row - b_row
        a_col = ig_col - b_col

        m_prev = m_ref[r]
        mx_col = jnp.maximum(jnp.max(jnp.where(causal, a_row, NEG_BIG), axis=1, keepdims=True), m_prev)
        mx_last = mx_col[L - 1:L, :]
        b_last = b_col[L - 1:L, :]

        dmat = jnp.where(causal, jnp.exp(jnp.minimum(a_row - mx_col, 0.0)), 0.0)
        sd = _dot_nt(q, k) * dmat
        inter = jnp.exp(m_prev - mx_col)
        ct = ct_ref[r]
        num = _dot(sd.astype(BF16), v) + inter * _dot(q, ct.astype(BF16))
        nq = jnp.sum(q.astype(F32) * n_ref[r], axis=1, keepdims=True)
        den = jnp.sum(sd, axis=1, keepdims=True) + inter * nq
        hh = num / jnp.maximum(jnp.abs(den), jnp.exp(-(b_col + mx_col)))

        hn = _rms(hh, hg_ref[h])
        out_ref[bb, :, vd] = (hn * _sigmoid(o_ref[bb, :, vd].astype(F32))).astype(out_ref.dtype)

        w_col = jnp.exp(a_col - mx_last)
        decay = jnp.exp(m_prev - mx_last)
        kw = k.astype(F32) * w_col
        ct_ref[r] = decay * ct + _dot(kw.T.astype(BF16), v)
        n_ref[r] = decay * n_ref[r] + jnp.sum(kw, axis=0, keepdims=True)
        m_ref[r] = b_last + mx_last


def _mlstm_core(q, k, v, o, gates_col, gates_row, head_gain):
    b, s, d = v.shape
    L = ML_CHUNK
    qk = ML_HEADS * ML_QK_DIM
    nb = ML_BATCH
    n_rec = nb * ML_HEADS
    return pl.pallas_call(
        _mlstm_kernel,
        grid=(b // nb, s // L),
        in_specs=[
            pl.BlockSpec((nb, L, qk), lambda bi, ci: (bi, ci, 0)),
            pl.BlockSpec((nb, L, qk), lambda bi, ci: (bi, ci, 0)),
            pl.BlockSpec((nb, L, d), lambda bi, ci: (bi, ci, 0)),
            pl.BlockSpec((nb, L, d), lambda bi, ci: (bi, ci, 0)),
            pl.BlockSpec((nb, L, LANES), lambda bi, ci: (bi, ci, 0)),
            pl.BlockSpec((nb, 2 * ML_HEADS, L), lambda bi, ci: (bi, 0, ci)),
            _const_spec((ML_HEADS, 1, ML_V_DIM)),
        ],
        out_specs=pl.BlockSpec((nb, L, d), lambda bi, ci: (bi, ci, 0)),
        out_shape=jax.ShapeDtypeStruct(v.shape, BF16),
        scratch_shapes=[pltpu.VMEM((n_rec, ML_QK_DIM, ML_V_DIM), F32),
                        pltpu.VMEM((n_rec, 1, ML_QK_DIM), F32),
                        pltpu.VMEM((n_rec, 1, 1), F32)],
        compiler_params=_params("parallel", "arbitrary"),
        name="mlstm_core",
    )(q, k, v, o, gates_col, gates_row, head_gain)


def _sb_inproj_kernel(x_ref, g_ref, w_ref, q_ref, k_ref, v_ref):
    xn = _rms(x_ref[...], g_ref[...]).astype(BF16)
    d = q_ref.shape[1]
    q_ref[...] = (_dot(xn, _w(w_ref, cols=slice(0, d))) * (LOG2_E * SB_HEAD_DIM ** -0.5)).astype(BF16)
    k_ref[...] = _dot(xn, _w(w_ref, cols=slice(d, 2 * d))).astype(BF16)
    v_ref[...] = _dot(xn, _w(w_ref, cols=slice(2 * d, 3 * d))).astype(BF16)


def _sb_inproj(x, g, w_qkv, layer):
    m, d = x.shape
    tm = ROW_TILE
    row = lambda n: pl.BlockSpec((tm, n), lambda i: (i, 0))
    return pl.pallas_call(
        _sb_inproj_kernel,
        grid=(m // tm,),
        in_specs=[row(d), _const_spec((1, d)), _layer_spec(w_qkv, layer)],
        out_specs=[row(d), row(d), row(d)],
        out_shape=[jax.ShapeDtypeStruct((m, d), BF16)] * 3,
        compiler_params=_params("parallel"),
        name="sb_inproj",
    )(x, g, w_qkv)


def _sb_kernel(q_ref, k_ref, v_ref, o_ref, z_ref, acc_ref, c_ref):
    qi = pl.program_id(2)
    T = SB_TILE
    lane = lax.broadcasted_iota(jnp.int32, (T, LANES), 1)
    first = lane < SB_HEAD_DIM
    row = lax.broadcasted_iota(jnp.int32, (T, T), 0)
    col = lax.broadcasted_iota(jnp.int32, (T, T), 1)
    neg_suffix = jnp.where(row >= col, -1.0, 0.0).astype(BF16)
    strict1 = col < row
    strict = jnp.concatenate([strict1, strict1], axis=0)

    cols = [slice(ch * LANES, (ch + 1) * LANES) for ch in range(SB_CHAINS)]
    qq = []
    for ch in range(SB_CHAINS):
        q = q_ref[0, :, cols[ch]]
        zero = jnp.zeros_like(q)
        qq.append(jnp.concatenate([jnp.where(first, q, zero), jnp.where(first, zero, q)], axis=0))

    def logits(idx, slot, diagonal=False):
        for ch in range(SB_CHAINS):
            kb = k_ref[0, pl.ds(pl.multiple_of(idx * T, T), T), cols[ch]]
            z = _rows2(lambda x: _dot_nt(x, kb), qq[ch])
            if diagonal:
                z = jnp.where(strict, z, NEG_BIG)
            z_ref[ch, slot] = z

    def weights(idx, slot):
        for ch in range(SB_CHAINS):
            vb = v_ref[0, pl.ds(pl.multiple_of(idx * T, T), T), cols[ch]]
            z = z_ref[ch, slot]
            sp = jnp.maximum(z, 0.0) + jnp.log2(1.0 + jnp.exp2(-jnp.abs(z)))
            cum = _rows2(lambda x: _dot(x, neg_suffix), sp.astype(BF16))
            carry = c_ref[ch]
            t = z + cum
            a = jnp.exp2(jnp.concatenate([t[:, :LANES] + carry, t[:, LANES:] + carry], axis=1))
            acc_ref[ch] += _rows2(lambda x: _dot(x, vb), a.astype(BF16))
            c_ref[ch] = carry + jnp.broadcast_to(cum[:, 0:1], carry.shape)

    acc_ref[...] = jnp.zeros_like(acc_ref)
    c_ref[...] = jnp.zeros_like(c_ref)
    odd = qi % 2

    @pl.when(odd == 0)
    def _():
        logits(qi, 0, diagonal=True)

    @pl.when(odd == 1)
    def _():
        logits(qi, 1, diagonal=True)
        logits(qi - 1, 0)
        weights(qi, 1)

    n_even = qi - odd

    def body(i, _):
        t = n_even - 2 * i
        logits(t - 1, 1)
        weights(t, 0)
        logits(t - 2, 0)
        weights(t - 1, 1)
        return 0

    lax.fori_loop(0, n_even // 2, body, 0)
    weights(0, 0)
    for ch in range(SB_CHAINS):
        acc = acc_ref[ch]
        o_ref[0, :, cols[ch]] = jnp.where(first, acc[:T], acc[T:]).astype(o_ref.dtype)


def _sb_core(q, k, v):
    b, s, d = q.shape
    T = SB_TILE
    w = SB_CHAINS * LANES
    return pl.pallas_call(
        _sb_kernel,
        grid=(b, d // w, s // T),
        in_specs=[pl.BlockSpec((1, T, w), lambda bi, hi, ti: (bi, ti, hi)),
                  pl.BlockSpec((1, s, w), lambda bi, hi, ti: (bi, 0, hi)),
                  pl.BlockSpec((1, s, w), lambda bi, hi, ti: (bi, 0, hi))],
        out_specs=pl.BlockSpec((1, T, w), lambda bi, hi, ti: (bi, ti, hi)),
        out_shape=jax.ShapeDtypeStruct(q.shape, BF16),
        scratch_shapes=[pltpu.VMEM((SB_CHAINS, 2, 2 * T, T), F32),
                        pltpu.VMEM((SB_CHAINS, 2 * T, LANES), F32),
                        pltpu.VMEM((SB_CHAINS, 2 * T, LANES), F32)],
        compiler_params=_params("parallel", "parallel", "arbitrary"),
        name="sb_core",
    )(q, k, v)


def _memkv_kernel(mem_ref, g_ref, w_ref, o_ref):
    mn = _rms(mem_ref[...], g_ref[...]).astype(BF16)
    o_ref[0] = _dot(mn, _w(w_ref)).astype(o_ref.dtype)


def _memkv(mem, g, w_kv):
    m, d = mem.shape
    depth, _, n = w_kv.shape
    tm = ROW_TILE
    return pl.pallas_call(
        _memkv_kernel,
        grid=(depth, m // tm),
        in_specs=[pl.BlockSpec((tm, d), lambda l, i: (i, 0)), _const_spec((1, d)),
                  pl.BlockSpec((1, d, n), lambda l, i: (l, 0, 0))],
        out_specs=pl.BlockSpec((1, tm, n), lambda l, i: (l, i, 0)),
        out_shape=jax.ShapeDtypeStruct((depth, m, n), BF16),
        compiler_params=_params("parallel", "parallel"),
        name="memkv",
    )(mem, g, w_kv)


def _xattn_kernel(a_ref, wmix_ref, gmix_ref, x_ref, gpre_ref, gpost_ref, wq_ref, kv_ref, wo_ref, o_ref):
    x = x_ref[0] + _rms(_dot(a_ref[0], _w(wmix_ref)), gmix_ref[...])
    xn = _rms(x, gpre_ref[...]).astype(BF16)
    q = (_dot(xn, _w(wq_ref)) * (XA_HEAD_DIM ** -0.5)).astype(BF16)
    d = XA_HEADS * XA_HEAD_DIM
    heads = []
    for h in range(XA_HEADS):
        lo, hi = h * XA_HEAD_DIM, (h + 1) * XA_HEAD_DIM
        s = _dot_nt(q[:, lo:hi], kv_ref[0, 0, :, lo:hi])
        p = jnp.exp(s - jnp.max(s, axis=-1, keepdims=True))
        denom = jnp.sum(p, axis=-1, keepdims=True)
        heads.append((_dot(p.astype(BF16), kv_ref[0, 0, :, d + lo:d + hi]) / denom).astype(BF16))
    attn = jnp.concatenate(heads, axis=-1)
    y = _dot(attn, _w(wo_ref))
    o_ref[0] = x + _rms(y, gpost_ref[...])


def _xattn(mixed, w_mix, mix_layer, gmix, x, gpre, gpost, wq, kv, layer, wo):
    b, s, d = x.shape
    n_mem = kv.shape[2]
    tm = ROW_TILE
    rows = pl.BlockSpec((1, tm, d), lambda bi, i: (bi, i, 0))
    return pl.pallas_call(
        _xattn_kernel,
        grid=(b, s // tm),
        in_specs=[rows, _layer_spec(w_mix, mix_layer), _const_spec((1, d)),
                  rows, _const_spec((1, d)), _const_spec((1, d)), _layer_spec(wq, layer),
                  pl.BlockSpec((1, 1, n_mem, 2 * d), lambda bi, i: (layer, bi, 0, 0)),
                  _layer_spec(wo, layer)],
        out_specs=rows,
        out_shape=jax.ShapeDtypeStruct(x.shape, F32),
        compiler_params=_params("parallel", "parallel"),
        name="xattn",
    )(mixed, w_mix, gmix, x, gpre, gpost, wq, kv, wo)


def _ffn_kernel(x_ref, gpre_ref, gpost_ref, wgu_ref, wd_ref, o_ref):
    x = x_ref[...]
    xn = _rms(x, gpre_ref[...]).astype(BF16)
    d_ff = wd_ref.shape[1]
    y = jnp.zeros(x.shape, F32)
    for c0 in range(0, d_ff, FF_CHUNK):
        gate = _dot(xn, _w(wgu_ref, cols=slice(c0, c0 + FF_CHUNK)))
        up = _dot(xn, _w(wgu_ref, cols=slice(d_ff + c0, d_ff + c0 + FF_CHUNK)))
        hid = (gate * _sigmoid(gate) * up).astype(BF16)
        y = y + _dot(hid, _w(wd_ref, rows=slice(c0, c0 + FF_CHUNK)))
    o_ref[...] = x + _rms(y, gpost_ref[...])


def _ffn(x, gpre, gpost, w_gate_up, w_down, layer):
    m, d = x.shape
    tm = FFN_ROW_TILE
    return pl.pallas_call(
        _ffn_kernel,
        grid=(m // tm,),
        in_specs=[pl.BlockSpec((tm, d), lambda i: (i, 0)), _const_spec((1, d)), _const_spec((1, d)),
                  _layer_spec(w_gate_up, layer), _layer_spec(w_down, layer)],
        out_specs=pl.BlockSpec((tm, d), lambda i: (i, 0)),
        out_shape=jax.ShapeDtypeStruct((m, d), F32),
        compiler_params=_params("parallel"),
        name="ffn",
    )(x, gpre, gpost, w_gate_up, w_down)


def kernel(x, mem, mem_norm_gain, norm_gains, ml_w_in, ml_b_gate, ml_head_gain, ml_w_out, sb_w_qkv, sb_w_out,
           xa_w_q, xa_w_kv, xa_w_o, ffn_w_gate_up, ffn_w_down):
    b, s, d = x.shape
    depth = norm_gains.shape[0]
    n_mem = mem.shape[1]
    m = b * s
    qk = ML_HEADS * ML_QK_DIM
    n_gates = 2 * ML_HEADS
    gains = norm_gains.reshape(depth, norm_gains.shape[1], 1, d)

    kv = _memkv(mem.reshape(b * n_mem, d), mem_norm_gain.reshape(1, d), xa_w_kv)
    kv = kv.reshape(depth, b, n_mem, 2 * d)

    xf = x.reshape(m, d)
    for layer in range(depth):
        g = gains[layer]
        j = layer // 2
        if layer % 2 == 0:
            wg = jnp.pad(ml_w_in[j, :, 2 * qk + 2 * d:], ((0, 0), (0, LANES - n_gates))).astype(BF16)
            bg = jnp.pad(ml_b_gate[j], (0, LANES - n_gates)).reshape(1, LANES)
            q, k, v, o, gates = _ml_inproj(xf, g[0], ml_w_in, j, wg, bg)
            gates = gates.reshape(b, s, LANES)
            gates_row = jnp.swapaxes(gates[:, :, :n_gates], 1, 2)
            mixed = _mlstm_core(q.reshape(b, s, qk), k.reshape(b, s, qk), v.reshape(b, s, d),
                                o.reshape(b, s, d), gates, gates_row,
                                ml_head_gain[j].reshape(ML_HEADS, 1, ML_V_DIM))
            w_mix = ml_w_out
        else:
            q, k, v = _sb_inproj(xf, g[0], sb_w_qkv, j)
            mixed = _sb_core(q.reshape(b, s, d), k.reshape(b, s, d), v.reshape(b, s, d))
            w_mix = sb_w_out
        xf = _xattn(mixed, w_mix, j, g[1], xf.reshape(b, s, d), g[2], g[3], xa_w_q, kv, layer,
                    xa_w_o).reshape(m, d)
        xf = _ffn(xf, g[4], g[5], ffn_w_gate_up, ffn_w_down, layer)
    return xf.reshape(b, s, d)
```
